```python
import math
import jax
import jax.numpy as jnp
from jax import lax
import numpy as np

D_MODEL = 1024
BATCH = 4
SEQ = 4096
DEPTH = 2
DEC_BATCH = 32
DEC_SEQ = 1
PAST_LEN = 16384
PAGE_SIZE = 128

N_AB = (DEPTH + 1) // 2
N_C = DEPTH // 2
SB_HEADS = 8
SB_HEAD_DIM = 64
SB_WIDTH = SB_HEADS * SB_HEAD_DIM
SB_BLOCK = 128
SB_LOGIT_BIAS_INIT = -8.0
GDN_HEADS = 4
GDN_DK = 128
GDN_DV = 128
GDN_WIDTH = GDN_HEADS * GDN_DV
GDN_CONV_DIM = GDN_HEADS * (2 * GDN_DK + GDN_DV)
CONV_W = 4
GDN_CHUNK = 64
AB_SIZES = (SB_WIDTH, SB_WIDTH, SB_WIDTH, SB_WIDTH, GDN_CONV_DIM, GDN_WIDTH, GDN_HEADS, GDN_HEADS)
AB_IN = 4 * SB_WIDTH + GDN_CONV_DIM + GDN_WIDTH + 2 * GDN_HEADS
AB_OUT = SB_WIDTH + GDN_WIDTH
RET_HEADS = 4
RET_DK = D_MODEL // RET_HEADS
RET_DV = 2 * D_MODEL // RET_HEADS
RET_WIDTH = RET_HEADS * RET_DV
RET_CHUNK = 128
ROPE_BASE = 10000.0
C_SIZES = (RET_HEADS * RET_DK, RET_HEADS * RET_DK, RET_WIDTH, RET_WIDTH)
C_IN = 2 * RET_HEADS * RET_DK + 2 * RET_WIDTH
MEM_LEN = 256
XA_HEADS = 4
XA_HEAD_DIM = D_MODEL // XA_HEADS
NORM_EPS = 1e-6

kernel_name = 'hybrid_stickbreak_gdn_retention_memxattn_step'


def rms_norm(x, g):
    xf = x.astype(jnp.float32)
    y = xf * lax.rsqrt(jnp.mean(xf * xf, axis=-1, keepdims=True) + NORM_EPS)
    return y.astype(x.dtype) * g


def l2_normalize(x):
    return x * lax.rsqrt(jnp.sum(x * x, axis=-1, keepdims=True) + NORM_EPS)


def split_last(x, sizes):
    return jnp.split(x, np.cumsum(sizes)[:-1].tolist(), axis=-1)


def to_chunks(a, c):
    b, t = a.shape[0], a.shape[1]
    a = a.reshape((b, t // c, c) + a.shape[2:])
    return jnp.moveaxis(a, (1, 2), (0, 3))


def from_chunks(a):
    n, b, h, c = a.shape[:4]
    return jnp.moveaxis(a, (0, 3), (1, 2)).reshape((b, n * c, h) + a.shape[4:])


def rotary(x, pos):
    half = x.shape[-1] // 2
    inv = ROPE_BASE ** (-jnp.arange(half, dtype=jnp.float32) / half)
    ang = pos.astype(jnp.float32)[:, None] * inv[None, :]
    cos = jnp.cos(ang)[None, :, None, :]
    sin = jnp.sin(ang)[None, :, None, :]
    x1, x2 = x[..., :half], x[..., half:]
    return jnp.concatenate([x1 * cos - x2 * sin, x2 * cos + x1 * sin], axis=-1)


def causal_conv(x_new, buf, w):
    xx = jnp.concatenate([buf.astype(x_new.dtype), x_new], axis=1)
    y = lax.conv_general_dilated(xx, w[:, None, :].astype(xx.dtype), window_strides=(1,), padding='VALID',
                                 dimension_numbers=('NWC', 'WIO', 'NWC'), feature_group_count=xx.shape[-1])
    return y, xx[:, xx.shape[1] - (CONV_W - 1):]


def stick_breaking_block(q, k, v, bias, q_pos, k_pos):
    z = (jnp.einsum('bqhd,bkhd->bhqk', q, k).astype(jnp.float32) * (SB_HEAD_DIM ** -0.5)
         + bias.astype(jnp.float32)[None, :, None, None])
    mask = k_pos[None, :] < q_pos[:, None]
    log_keep = jnp.where(mask, jax.nn.log_sigmoid(-z), 0.0)
    between = lax.cumsum(log_keep, axis=3, reverse=True) - log_keep
    a = jnp.where(mask, jnp.exp(jax.nn.log_sigmoid(z) + between), 0.0)
    return jnp.einsum('bhqk,bkhd->bqhd', a.astype(v.dtype), v)


def stick_breaking(q, k_all, v_all, bias, q_pos):
    b, t, h, d = q.shape
    k_pos = jnp.arange(k_all.shape[1])
    if t <= SB_BLOCK:
        return stick_breaking_block(q, k_all, v_all, bias, q_pos, k_pos)
    nb = t // SB_BLOCK
    qb = jnp.moveaxis(q.reshape(b, nb, SB_BLOCK, h, d), 1, 0)
    pb = q_pos.reshape(nb, SB_BLOCK)
    ob = lax.map(lambda xs: stick_breaking_block(xs[0], k_all, v_all, bias, xs[1], k_pos), (qb, pb))
    return jnp.moveaxis(ob, 0, 1).reshape(b, t, h, d)


def gated_delta_chunked(q, k, v, g, beta, s0, chunk):
    qc, kc, vc = to_chunks(q, chunk), to_chunks(k, chunk), to_chunks(v, chunk)
    gc = jnp.cumsum(to_chunks(g, chunk), axis=-1)
    bc = to_chunks(beta, chunk)
    tril = jnp.tril(jnp.ones((chunk, chunk), dtype=bool))
    stril = jnp.tril(jnp.ones((chunk, chunk), dtype=bool), -1)
    decay = jnp.exp(jnp.where(tril, gc[..., :, None] - gc[..., None, :], -jnp.inf))
    kb = kc * bc[..., None]
    eye = jnp.eye(chunk, dtype=jnp.float32)
    lower = eye + jnp.where(stril, jnp.einsum('nbhid,nbhjd->nbhij', kb, kc) * decay, 0.0)
    tmat = lax.linalg.triangular_solve(lower, jnp.broadcast_to(eye, lower.shape), left_side=True,
                                       lower=True, unit_diagonal=True)
    u = jnp.einsum('nbhij,nbhje->nbhie', tmat, vc * bc[..., None])
    w = jnp.einsum('nbhij,nbhjd->nbhid', tmat, kb * jnp.exp(gc)[..., None])

    def step(s, xs):
        qi, ki, ui, wi, gi, di = xs
        v_new = ui - jnp.einsum('bhcd,bhde->bhce', wi, s)
        attn = jnp.einsum('bhid,bhjd->bhij', qi, ki) * di
        o = (jnp.einsum('bhcd,bhde->bhce', qi * jnp.exp(gi)[..., None], s)
             + jnp.einsum('bhij,bhje->bhie', attn, v_new))
        g_last = gi[..., -1:]
        s = (s * jnp.exp(g_last)[..., None]
             + jnp.einsum('bhcd,bhce->bhde', ki * jnp.exp(g_last - gi)[..., None], v_new))
        return s, o

    s, o = lax.scan(step, s0, (qc, kc, u, w, gc, decay))
    return from_chunks(o), s


def retention_chunked(q, k, v, s0, chunk):
    h = q.shape[2]
    lg = jnp.log1p(-jnp.exp2(-5.0 - jnp.arange(h, dtype=jnp.float32)))
    idx = jnp.arange(chunk, dtype=jnp.float32)
    diff = idx[:, None] - idx[None, :]
    dmask = jnp.where(diff >= 0, jnp.exp(lg[:, None, None] * jnp.maximum(diff, 0.0)), 0.0)
    xi = jnp.exp(lg[:, None] * (idx + 1.0))[..., None]
    zeta = jnp.exp(lg[:, None] * (chunk - 1.0 - idx))[..., None]
    g_chunk = jnp.exp(lg * chunk)[:, None, None]
    qc, kc, vc = to_chunks(q, chunk), to_chunks(k, chunk), to_chunks(v, chunk)

    def step(s, xs):
        qi, ki, vi = xs
        inner = jnp.einsum('bhid,bhjd->bhij', qi, ki) * dmask
        o = jnp.einsum('bhij,bhje->bhie', inner, vi) + jnp.einsum('bhcd,bhde->bhce', qi * xi, s)
        s = g_chunk * s + jnp.einsum('bhcd,bhce->bhde', ki * zeta, vi)
        return s, o

    s, o = lax.scan(step, s0, (qc, kc, vc))
    return from_chunks(o), s


def mem_xattn(h, mk, mv, wq, wo):
    b, t, _ = h.shape
    q = (h @ wq).reshape(b, t, XA_HEADS, XA_HEAD_DIM)
    s = jnp.einsum('bthd,bmhd->bhtm', q, mk).astype(jnp.float32) * (XA_HEAD_DIM ** -0.5)
    p = jax.nn.softmax(s, axis=-1).astype(mv.dtype)
    o = jnp.einsum('bhtm,bmhd->bthd', p, mv).reshape(b, t, XA_HEADS * XA_HEAD_DIM)
    return o @ wo


def ab_mixer(h, pos, past_k, past_v, conv_buf, s0, w_in, sb_bias, conv_w, a_log, dt_bias, norm_g, w_out):
    b, t, _ = h.shape
    sb_q, sb_k, sb_v, sb_z, qkv, g_z, g_a, g_b = split_last(h @ w_in, AB_SIZES)
    shp = (b, t, SB_HEADS, SB_HEAD_DIM)
    k_rows, v_rows = sb_k.reshape(shp), sb_v.reshape(shp)
    k_all = jnp.concatenate([past_k.astype(k_rows.dtype), k_rows], axis=1)
    v_all = jnp.concatenate([past_v.astype(v_rows.dtype), v_rows], axis=1)
    o_sb = stick_breaking(sb_q.reshape(shp), k_all, v_all, sb_bias, pos).reshape(b, t, SB_WIDTH) * jax.nn.silu(sb_z)
    conv_out, conv_tail = causal_conv(qkv, conv_buf, conv_w)
    gq, gk, gv = jnp.split(jax.nn.silu(conv_out).astype(jnp.float32), [GDN_HEADS * GDN_DK, 2 * GDN_HEADS * GDN_DK], axis=-1)
    gq = l2_normalize(gq.reshape(b, t, GDN_HEADS, GDN_DK)) * (GDN_DK ** -0.5)
    gk = l2_normalize(gk.reshape(b, t, GDN_HEADS, GDN_DK))
    gv = gv.reshape(b, t, GDN_HEADS, GDN_DV)
    beta = jax.nn.sigmoid(g_b.astype(jnp.float32))
    g = -jnp.exp(a_log.astype(jnp.float32)) * jax.nn.softplus(g_a.astype(jnp.float32) + dt_bias.astype(jnp.float32))
    o_g, s_new = gated_delta_chunked(gq, gk, gv, g, beta, s0.astype(jnp.float32), min(GDN_CHUNK, t))
    o_g = rms_norm(o_g, norm_g.astype(jnp.float32)).reshape(b, t, GDN_WIDTH).astype(h.dtype) * jax.nn.silu(g_z)
    y = jnp.concatenate([o_sb, o_g], axis=-1) @ w_out
    return y, k_rows, v_rows, conv_tail, s_new.astype(s0.dtype)


def c_mixer(h, pos, s0, w_in, norm_g, w_out):
    b, t, _ = h.shape
    q, k, v, z = split_last(h @ w_in, C_SIZES)
    q = rotary(q.reshape(b, t, RET_HEADS, RET_DK).astype(jnp.float32), pos)
    k = rotary(k.reshape(b, t, RET_HEADS, RET_DK).astype(jnp.float32), pos) * (RET_DK ** -0.5)
    v = v.reshape(b, t, RET_HEADS, RET_DV).astype(jnp.float32)
    o, s_new = retention_chunked(q, k, v, s0.astype(jnp.float32), min(RET_CHUNK, t))
    o = rms_norm(o, norm_g.astype(jnp.float32)).reshape(b, t, RET_WIDTH).astype(h.dtype) * jax.nn.silu(z)
    return o @ w_out, s_new.astype(s0.dtype)


def trunk(x, sb_past_k, sb_past_v, conv_bufs, gdn_states, ret_states, mem_k, mem_v,
          norm_mix_g, norm_xa_g, w_in_ab, sb_logit_bias, conv_w_gdn, gdn_a_log, gdn_dt_bias, gdn_norm_g, w_out_ab,
          w_in_c, ret_norm_g, w_out_c, xa_wq, xa_wo, norm_final_g):
    pos = sb_past_k.shape[2] + jnp.arange(x.shape[1])
    ks, vs, convs, gdns, rets = [], [], [], [], []
    for layer in range(DEPTH):
        i = layer // 2
        h = rms_norm(x, norm_mix_g[layer])
        if layer % 2 == 0:
            y, k_rows, v_rows, conv_tail, s_gdn = ab_mixer(
                h, pos, sb_past_k[i], sb_past_v[i], conv_bufs[i], gdn_states[i], w_in_ab[i], sb_logit_bias[i],
                conv_w_gdn[i], gdn_a_log[i], gdn_dt_bias[i], gdn_norm_g[i], w_out_ab[i])
            ks.append(k_rows)
            vs.append(v_rows)
            convs.append(conv_tail)
            gdns.append(s_gdn)
        else:
            y, s_ret = c_mixer(h, pos, ret_states[i], w_in_c[i], ret_norm_g[i], w_out_c[i])
            rets.append(s_ret)
        x = x + y
        x = x + mem_xattn(rms_norm(x, norm_xa_g[layer]), mem_k[layer], mem_v[layer], xa_wq[layer], xa_wo[layer])
    return (rms_norm(x, norm_final_g), jnp.stack(ks), jnp.stack(vs), jnp.stack(gdns), jnp.stack(convs), jnp.stack(rets))


def setup_inputs(seed: int = 0) -> dict:
    key = jax.random.key(seed)
    keys = iter(jax.random.split(key, 40))
    f32 = jnp.float32
    n_pages = PAST_LEN // PAGE_SIZE
    n_phys = (DEC_BATCH * n_pages * 5) // 4

    def nrm(shape, scale):
        return jax.random.normal(next(keys), shape, f32) * scale

    inp = {}
    inp['x_prompt'] = nrm((BATCH, SEQ, D_MODEL), 1.0)
    inp['x_sample'] = nrm((DEC_BATCH, DEC_SEQ, D_MODEL), 1.0)
    inp['mem_prompt'] = nrm((BATCH, MEM_LEN, D_MODEL), 1.0)
    inp['cache_sb_k'] = nrm((N_AB, n_phys, PAGE_SIZE, SB_HEADS, SB_HEAD_DIM), 1.0)
    inp['cache_sb_v'] = nrm((N_AB, n_phys, PAGE_SIZE, SB_HEADS, SB_HEAD_DIM), 1.0)
    inp['state_gdn'] = nrm((N_AB, DEC_BATCH, GDN_HEADS, GDN_DK, GDN_DV), GDN_DK ** -0.5)
    inp['state_gdn_conv'] = nrm((N_AB, DEC_BATCH, CONV_W - 1, GDN_CONV_DIM), 1.0)
    inp['state_ret'] = nrm((N_C, DEC_BATCH, RET_HEADS, RET_DK, RET_DV), 0.5)
    inp['cache_mem_k'] = nrm((DEPTH, DEC_BATCH, MEM_LEN, XA_HEADS, XA_HEAD_DIM), 1.0)
    inp['cache_mem_v'] = nrm((DEPTH, DEC_BATCH, MEM_LEN, XA_HEADS, XA_HEAD_DIM), 1.0)
    perm = jax.random.permutation(next(keys), n_phys)
    inp['page_table'] = perm[:DEC_BATCH * n_pages].reshape(DEC_BATCH, n_pages).astype(jnp.int32)
    inp['norm_mix_g'] = 1.0 + nrm((DEPTH, D_MODEL), 0.02)
    inp['norm_xa_g'] = 1.0 + nrm((DEPTH, D_MODEL), 0.02)
    inp['w_in_ab'] = nrm((N_AB, D_MODEL, AB_IN), D_MODEL ** -0.5)
    inp['sb_logit_bias'] = SB_LOGIT_BIAS_INIT + nrm((N_AB, SB_HEADS), 0.25)
    inp['conv_w_gdn'] = nrm((N_AB, CONV_W, GDN_CONV_DIM), CONV_W ** -0.5)
    inp['gdn_a_log'] = jnp.log(jax.random.uniform(next(keys), (N_AB, GDN_HEADS), f32, 1.0, 16.0))
    dt = jnp.exp(jax.random.uniform(next(keys), (N_AB, GDN_HEADS), f32, math.log(1e-3), math.log(1e-1)))
    inp['gdn_dt_bias'] = dt + jnp.log(-jnp.expm1(-dt))
    inp['gdn_norm_g'] = 1.0 + nrm((N_AB, GDN_DV), 0.02)
    inp['w_out_ab'] = nrm((N_AB, AB_OUT, D_MODEL), AB_OUT ** -0.5)
    inp['w_in_c'] = nrm((N_C, D_MODEL, C_IN), D_MODEL ** -0.5)
    inp['ret_norm_g'] = 1.0 + nrm((N_C, RET_DV), 0.02)
    inp['w_out_c'] = nrm((N_C, RET_WIDTH, D_MODEL), RET_WIDTH ** -0.5)
    inp['xa_wq'] = nrm((DEPTH, D_MODEL, D_MODEL), D_MODEL ** -0.5)
    inp['xa_wk'] = nrm((DEPTH, D_MODEL, D_MODEL), D_MODEL ** -0.5)
    inp['xa_wv'] = nrm((DEPTH, D_MODEL, D_MODEL), D_MODEL ** -0.5)
    inp['xa_wo'] = nrm((DEPTH, D_MODEL, D_MODEL), D_MODEL ** -0.5)
    inp['norm_final_g'] = 1.0 + nrm((D_MODEL,), 0.02)
    return inp


def reference(x_prompt, x_sample, mem_prompt, cache_sb_k, cache_sb_v, state_gdn, state_gdn_conv, state_ret,
              cache_mem_k, cache_mem_v, page_table, norm_mix_g, norm_xa_g, w_in_ab, sb_logit_bias, conv_w_gdn,
              gdn_a_log, gdn_dt_bias, gdn_norm_g, w_out_ab, w_in_c, ret_norm_g, w_out_c, xa_wq, xa_wk, xa_wv,
              xa_wo, norm_final_g):
    bp, mlen = mem_prompt.shape[0], mem_prompt.shape[1]
    mem_k_prompt = jnp.einsum('bmd,lde->lbme', mem_prompt, xa_wk).reshape(DEPTH, bp, mlen, XA_HEADS, XA_HEAD_DIM)
    mem_v_prompt = jnp.einsum('bmd,lde->lbme', mem_prompt, xa_wv).reshape(DEPTH, bp, mlen, XA_HEADS, XA_HEAD_DIM)
    dt = x_prompt.dtype
    empty_kv = jnp.zeros((N_AB, bp, 0, SB_HEADS, SB_HEAD_DIM), dt)
    conv0 = jnp.zeros((N_AB, bp, CONV_W - 1, GDN_CONV_DIM), dt)
    gdn0 = jnp.zeros((N_AB, bp, GDN_HEADS, GDN_DK, GDN_DV), dt)
    ret0 = jnp.zeros((N_C, bp, RET_HEADS, RET_DK, RET_DV), dt)
    y_prompt, sb_k_prompt, sb_v_prompt, gdn_state_prompt, gdn_conv_prompt, ret_state_prompt = trunk(
        x_prompt, empty_kv, empty_kv, conv0, gdn0, ret0, mem_k_prompt, mem_v_prompt,
        norm_mix_g, norm_xa_g, w_in_ab, sb_logit_bias, conv_w_gdn, gdn_a_log, gdn_dt_bias, gdn_norm_g, w_out_ab,
        w_in_c, ret_norm_g, w_out_c, xa_wq, xa_wo, norm_final_g)
    n_seq, n_pages = page_table.shape
    past_k = cache_sb_k[:, page_table].reshape(N_AB, n_seq, n_pages * PAGE_SIZE, SB_HEADS, SB_HEAD_DIM)
    past_v = cache_sb_v[:, page_table].reshape(N_AB, n_seq, n_pages * PAGE_SIZE, SB_HEADS, SB_HEAD_DIM)
    y_sample, sb_k_sample, sb_v_sample, gdn_state_sample, gdn_conv_sample, ret_state_sample = trunk(
        x_sample, past_k, past_v, state_gdn_conv, state_gdn, state_ret, cache_mem_k, cache_mem_v,
        norm_mix_g, norm_xa_g, w_in_ab, sb_logit_bias, conv_w_gdn, gdn_a_log, gdn_dt_bias, gdn_norm_g, w_out_ab,
        w_in_c, ret_norm_g, w_out_c, xa_wq, xa_wo, norm_final_g)
    return (y_prompt, y_sample, sb_k_prompt, sb_v_prompt, gdn_state_prompt, gdn_conv_prompt, ret_state_prompt,
            mem_k_prompt, mem_v_prompt, sb_k_sample, sb_v_sample, gdn_state_sample, gdn_conv_sample, ret_state_sample)
```

```python
import functools
import math

import numpy as np
import jax
import jax.numpy as jnp
from jax import lax
from jax.experimental import pallas as pl
from jax.experimental.pallas import tpu as pltpu

F32 = jnp.float32
BF16 = jnp.bfloat16

SB_HEADS = 8
SB_HEAD_DIM = 64
SB_WIDTH = SB_HEADS * SB_HEAD_DIM
SB_TILE = 128
GDN_HEADS = 4
GDN_DK = 128
GDN_DV = 128
GDN_WIDTH = GDN_HEADS * GDN_DV
GDN_CONV_DIM = GDN_HEADS * (2 * GDN_DK + GDN_DV)
CONV_W = 4
GDN_CHUNK = 64
AB_IN = 4 * SB_WIDTH + GDN_CONV_DIM + GDN_WIDTH + 2 * GDN_HEADS
RET_HEADS = 4
RET_CHUNK = 128
ROPE_BASE = 10000.0
XA_HEADS = 4
NORM_EPS = 1e-6
LANES = 128
SUBLANES = 8
VMEM_LIMIT = 56 * 1024 * 1024


def _cparams(*sem):
    return pltpu.CompilerParams(dimension_semantics=sem, vmem_limit_bytes=VMEM_LIMIT)


def _rms(x, g):
    ms = jnp.mean(x * x, axis=-1, keepdims=True)
    return x * lax.rsqrt(ms + NORM_EPS) * g


def _silu(x):
    return x * jax.nn.sigmoid(x)


def _softplus(x):
    return jnp.maximum(x, 0.0) + jnp.log1p(jnp.exp(-jnp.abs(x)))


def _dot(a, b):
    return jnp.dot(a.astype(BF16), b.astype(BF16), preferred_element_type=F32)


def _dot_nt(a, b):
    return lax.dot_general(a.astype(BF16), b.astype(BF16), (((1,), (1,)), ((), ())),
                           preferred_element_type=F32)


def _dot_tn(a, b):
    return lax.dot_general(a.astype(BF16), b.astype(BF16), (((0,), (0,)), ((), ())),
                           preferred_element_type=F32)


def _split_bf16(x):
    hi = x.astype(BF16)
    lo = (x - hi.astype(F32)).astype(BF16)
    return hi, lo


def _proj_kernel(x_ref, g_ref, w_ref, o_ref, h_ref, *, norm):
    @pl.when(pl.program_id(1) == 0)
    def _():
        x = x_ref[...]
        if norm:
            x = _rms(x, g_ref[...])
        h_ref[...] = x.astype(BF16)

    o_ref[...] = jnp.dot(h_ref[...], w_ref[...], preferred_element_type=F32)


def _pick_tile(n, cap, unit):
    best = unit
    for t in range(unit, min(n, cap) + 1, unit):
        if n % t == 0:
            best = t
    return best


def _proj(x2d, g, w_bf16, *, norm):
    m, d = x2d.shape
    n = w_bf16.shape[1]
    tm = m if m <= 512 else _pick_tile(m, 512, SUBLANES)
    tn = _pick_tile(n, 2048, LANES)
    return pl.pallas_call(
        functools.partial(_proj_kernel, norm=norm),
        out_shape=jax.ShapeDtypeStruct((m, n), F32),
        grid=(m // tm, n // tn),
        in_specs=[pl.BlockSpec((tm, d), lambda i, j: (i, 0)),
                  pl.BlockSpec((1, d), lambda i, j: (0, 0)),
                  pl.BlockSpec((d, tn), lambda i, j: (0, j))],
        out_specs=pl.BlockSpec((tm, tn), lambda i, j: (i, j)),
        scratch_shapes=[pltpu.VMEM((tm, d), BF16)],
        compiler_params=_cparams("parallel", "arbitrary"),
        name="proj",
    )(x2d, g.reshape(1, d), w_bf16)


def _suffix_matrix():
    j = np.arange(SB_TILE)[:, None]
    s = np.arange(SB_TILE)[None, :]
    u = (j >= s).astype(np.float32)
    return jnp.asarray(np.concatenate([u, np.ones_like(u)], axis=1), dtype=BF16)


def _sb_tile(qh, kb, vb, tri, bias, c, o, mask):
    z = _dot_nt(qh, kb) + bias
    lk = -_softplus(z)
    if mask is not None:
        lk = jnp.where(mask, lk, 0.0)
    hi, lo = _split_bf16(lk)
    r = (jnp.dot(hi, tri, preferred_element_type=F32) + jnp.dot(lo, tri, preferred_element_type=F32))
    incl = r[:, :SB_TILE]
    tot = r[:, SB_TILE:]
    a = jnp.exp(z + incl + c)
    if mask is not None:
        a = jnp.where(mask, a, 0.0)
    o = o + jnp.dot(a.astype(BF16), vb, preferred_element_type=F32)
    return c + tot, o


def _sb_prompt_kernel(bias_ref, q_ref, k_ref, v_ref, z_ref, tri_ref, o_ref):
    hp = pl.program_id(1)
    i = pl.program_id(2)
    tq = q_ref.shape[1]
    q = q_ref[0] * (SB_HEAD_DIM ** -0.5)
    lane = lax.broadcasted_iota(jnp.int32, (tq, LANES), 1)
    first = lane < SB_HEAD_DIM
    qs = (jnp.where(first, q, 0.0).astype(BF16), jnp.where(first, 0.0, q).astype(BF16))
    biases = (bias_ref[2 * hp], bias_ref[2 * hp + 1])
    tri = tri_ref[...]
    row = lax.broadcasted_iota(jnp.int32, (tq, SB_TILE), 0)
    col = lax.broadcasted_iota(jnp.int32, (tq, SB_TILE), 1)

    def tile(j, carry, mask):
        start = pl.multiple_of(j * SB_TILE, SB_TILE)
        kb = k_ref[0, pl.ds(start, SB_TILE), :].astype(BF16)
        vb = v_ref[0, pl.ds(start, SB_TILE), :].astype(BF16)
        out = []
        for hh in range(2):
            c, o = carry[hh]
            out.append(_sb_tile(qs[hh], kb, vb, tri, biases[hh], c, o, mask))
        return tuple(out)

    zero = jnp.zeros((tq, SB_TILE), F32)
    carry = ((zero, zero), (zero, zero))
    carry = tile(i, carry, col < row)
    carry = lax.fori_loop(0, i, lambda jj, cr: tile(i - 1 - jj, cr, None), carry)
    o = jnp.where(first, carry[0][1], carry[1][1])
    o_ref[0] = o * _silu(z_ref[0])


def _sb_prompt(p3, bias):
    b, t, _ = p3.shape
    tq = SB_TILE
    pairs = SB_WIDTH // LANES
    return pl.pallas_call(
        _sb_prompt_kernel,
        out_shape=jax.ShapeDtypeStruct((b, t, SB_WIDTH), F32),
        grid=(b, pairs, t // tq),
        in_specs=[pl.BlockSpec(memory_space=pltpu.SMEM),
                  pl.BlockSpec((1, tq, LANES), lambda bi, hp, i: (bi, i, hp)),
                  pl.BlockSpec((1, t, LANES), lambda bi, hp, i: (bi, 0, pairs + hp)),
                  pl.BlockSpec((1, t, LANES), lambda bi, hp, i: (bi, 0, 2 * pairs + hp)),
                  pl.BlockSpec((1, tq, LANES), lambda bi, hp, i: (bi, i, 3 * pairs + hp)),
                  pl.BlockSpec((SB_TILE, 2 * SB_TILE), lambda bi, hp, i: (0, 0))],
        out_specs=pl.BlockSpec((1, tq, LANES), lambda bi, hp, i: (bi, i, hp)),
        compiler_params=_cparams("parallel", "parallel", "arbitrary"),
        name="sb_prompt",
    )(bias, p3, p3, p3, p3, _suffix_matrix())


SB_PAGES_PER_STEP = 8


def _sb_decode_kernel(pt_ref, q_ref, z_ref, bias_ref, tri_ref, *rest):
    del pt_ref
    g = SB_PAGES_PER_STEP
    k_refs = rest[:g]
    v_refs = rest[g:2 * g]
    o_ref = rest[2 * g]
    c_ref, acc_ref = rest[2 * g + 1:]
    step = pl.program_id(1)

    @pl.when(step == 0)
    def _():
        c_ref[...] = jnp.zeros_like(c_ref)
        acc_ref[...] = jnp.zeros_like(acc_ref)

    q = q_ref[0][0:1, :] * (SB_HEAD_DIM ** -0.5)
    row = lax.broadcasted_iota(jnp.int32, (SB_HEADS, SB_WIDTH), 0)
    lane = lax.broadcasted_iota(jnp.int32, (SB_HEADS, SB_WIDTH), 1)
    own = lane // SB_HEAD_DIM == row
    qbd = jnp.where(own, jnp.broadcast_to(q, (SB_HEADS, SB_WIDTH)), 0.0).astype(BF16)
    bias = bias_ref[...]
    tri = tri_ref[...]
    c = c_ref[...]
    acc = acc_ref[...]
    for i in range(g):
        kb = k_refs[i][0].astype(BF16)
        vb = v_refs[i][0].astype(BF16)
        c, acc = _sb_tile(qbd, kb, vb, tri, bias, c, acc, None)
    c_ref[...] = c
    acc_ref[...] = acc

    @pl.when(step == pl.num_programs(1) - 1)
    def _():
        o = jnp.sum(jnp.where(own, acc, 0.0), axis=0, keepdims=True)
        rows = o_ref.shape[1]
        rid = lax.broadcasted_iota(jnp.int32, (rows, SB_WIDTH), 0)
        o_ref[0] = jnp.where(rid == 0, jnp.broadcast_to(o, (rows, SB_WIDTH)), 0.0) * _silu(z_ref[0])


def _sb_decode(p3, bias, cache_k, cache_v, page_table):
    b, rows, _ = p3.shape
    n_pages = page_table.shape[1]
    page = cache_k.shape[1]
    g = SB_PAGES_PER_STEP
    assert n_pages % g == 0 and page == SB_TILE
    steps = n_pages // g

    def page_spec(i):
        return pl.BlockSpec((1, page, SB_WIDTH), lambda bi, s, pt: (pt[bi, n_pages - 1 - (s * g + i)], 0, 0))

    bias_b = jnp.broadcast_to(bias.astype(F32)[:, None], (SB_HEADS, SB_TILE))
    grid_spec = pltpu.PrefetchScalarGridSpec(
        num_scalar_prefetch=1,
        grid=(b, steps),
        in_specs=[pl.BlockSpec((1, rows, SB_WIDTH), lambda bi, s, pt: (bi, 0, 0)),
                  pl.BlockSpec((1, rows, SB_WIDTH), lambda bi, s, pt: (bi, 0, 3)),
                  pl.BlockSpec((SB_HEADS, SB_TILE), lambda bi, s, pt: (0, 0)),
                  pl.BlockSpec((SB_TILE, 2 * SB_TILE), lambda bi, s, pt: (0, 0))]
        + [page_spec(i) for i in range(g)] + [page_spec(i) for i in range(g)],
        out_specs=pl.BlockSpec((1, rows, SB_WIDTH), lambda bi, s, pt: (bi, 0, 0)),
        scratch_shapes=[pltpu.VMEM((SB_HEADS, SB_TILE), F32), pltpu.VMEM((SB_HEADS, SB_WIDTH), F32)],
    )
    return pl.pallas_call(
        _sb_decode_kernel,
        out_shape=jax.ShapeDtypeStruct((b, rows, SB_WIDTH), F32),
        grid_spec=grid_spec,
        compiler_params=_cparams("parallel", "arbitrary"),
        name="sb_decode",
    )(page_table, p3, p3, bias_b, _suffix_matrix(), *([cache_k] * g), *([cache_v] * g))


def _gdn_kernel(hp_ref, q_ref, k_ref, v_ref, gz_ref, gate_ref, cw_ref, ng_ref, buf_ref, s0_ref,
                o_ref, sout_ref, s_ref, tail_ref, *, t_true):
    n = pl.program_id(1)
    c = q_ref.shape[1]
    hw = GDN_HEADS * GDN_DK

    @pl.when(n == 0)
    def _():
        s_ref[...] = s0_ref[0]
        tail_ref[...] = buf_ref[0]

    rowid = lax.broadcasted_iota(jnp.int32, (c, 1), 0) + n * c
    valid = (rowid < t_true).astype(F32)
    cw = cw_ref[...]
    tail = tail_ref[...]

    def conv(x_ref, sec):
        x = x_ref[0]
        xx = jnp.concatenate([tail[:, sec * hw:(sec + 1) * hw], x], axis=0)
        w = cw[:, sec * hw:(sec + 1) * hw]
        y = w[CONV_W - 1:CONV_W] * x
        for i in range(CONV_W - 1):
            off = SUBLANES - (CONV_W - 1) + i
            y = y + w[i:i + 1] * xx[off:off + c]
        tail_ref[:, sec * hw:(sec + 1) * hw] = x[c - SUBLANES:, :]
        return _silu(y) * valid

    yq, yk, yv = conv(q_ref, 0), conv(k_ref, 1), conv(v_ref, 2)
    gates = gate_ref[0]
    gz = gz_ref[0]
    ii = lax.broadcasted_iota(jnp.int32, (c, c), 0)
    jj = lax.broadcasted_iota(jnp.int32, (c, c), 1)
    lower_incl = (jj <= ii).astype(BF16)
    upper_incl = ii <= jj
    ones8 = jnp.ones((SUBLANES, c), BF16)
    eye = (ii == jj).astype(F32)

    for h in range(GDN_HEADS):
        sl = slice(h * GDN_DK, (h + 1) * GDN_DK)
        qh, kh, vh = yq[:, sl], yk[:, sl], yv[:, sl]
        qh = qh * lax.rsqrt(jnp.sum(qh * qh, axis=-1, keepdims=True) + NORM_EPS) * (GDN_DK ** -0.5)
        kh = kh * lax.rsqrt(jnp.sum(kh * kh, axis=-1, keepdims=True) + NORM_EPS)
        beta = jax.nn.sigmoid(gates[:, GDN_HEADS + h:GDN_HEADS + h + 1]) * valid
        a_log = hp_ref[0:1, h:h + 1]
        dt_bias = hp_ref[1:2, h:h + 1]
        gl = -jnp.exp(a_log) * _softplus(gates[:, h:h + 1] + dt_bias) * valid
        gb = jnp.broadcast_to(gl, (c, c))
        g_hi, g_lo = _split_bf16(gb)
        gcol = (jnp.dot(lower_incl, g_hi, preferred_element_type=F32)
                + jnp.dot(lower_incl, g_lo, preferred_element_type=F32))
        u_hi, u_lo = _split_bf16(jnp.where(upper_incl, gb, 0.0))
        grow = (jnp.dot(ones8, u_hi, preferred_element_type=F32)
                + jnp.dot(ones8, u_lo, preferred_element_type=F32))[0:1, :]
        decay = jnp.where(ii >= jj, jnp.exp(gcol - grow), 0.0)
        gc = gcol[:, 0:1]
        g_last = gcol[c - 1:c, 0:1]
        e_gc = jnp.exp(gc)
        kb = kh * beta
        m = jnp.where(ii > jj, _dot_nt(kb, kh) * decay, 0.0)
        tinv = eye - m
        pw = m
        k = 2
        while k < c:
            pw = _dot(pw, pw)
            tinv = tinv + _dot(tinv, pw)
            k *= 2
        u = _dot(tinv, vh * beta)
        w = _dot(tinv, kb * e_gc)
        s_old = s_ref[h]
        s_b = s_old.astype(BF16)
        v_new = u - _dot(w, s_b)
        attn = _dot_nt(qh, kh) * decay
        o = _dot(qh * e_gc, s_b) + _dot(attn, v_new)
        s_ref[h] = s_old * jnp.exp(g_last) + _dot_tn(kh * jnp.exp(g_last - gc), v_new)
        o = _rms(o, ng_ref[...]) * _silu(gz[:, h * GDN_DV:(h + 1) * GDN_DV])
        o_ref[0, :, h * GDN_DV:(h + 1) * GDN_DV] = o

    @pl.when(n == pl.num_programs(1) - 1)
    def _():
        sout_ref[0] = s_ref[...]


def _gdn(p3, conv_w, a_log, dt_bias, norm_g, buf8, s0, t_true):
    b, tp, _ = p3.shape
    c = min(GDN_CHUNK, tp)
    hw = GDN_HEADS * GDN_DK
    base = 4 * SB_WIDTH // hw
    kern = functools.partial(_gdn_kernel, t_true=t_true)
    head_params = jnp.pad(jnp.stack([a_log, dt_bias]).astype(F32), ((0, 0), (0, LANES - GDN_HEADS)))
    return pl.pallas_call(
        kern,
        out_shape=(jax.ShapeDtypeStruct((b, tp, GDN_WIDTH), F32),
                   jax.ShapeDtypeStruct((b, GDN_HEADS, GDN_DK, GDN_DV), F32)),
        grid=(b, tp // c),
        in_specs=[pl.BlockSpec((2, LANES), lambda bi, n: (0, 0)),
                  pl.BlockSpec((1, c, hw), lambda bi, n: (bi, n, base)),
                  pl.BlockSpec((1, c, hw), lambda bi, n: (bi, n, base + 1)),
                  pl.BlockSpec((1, c, hw), lambda bi, n: (bi, n, base + 2)),
                  pl.BlockSpec((1, c, GDN_WIDTH), lambda bi, n: (bi, n, base + 3)),
                  pl.BlockSpec((1, c, LANES), lambda bi, n: (bi, n, (4 * SB_WIDTH + GDN_CONV_DIM + GDN_WIDTH) // LANES)),
                  pl.BlockSpec((CONV_W, GDN_CONV_DIM), lambda bi, n: (0, 0)),
                  pl.BlockSpec((1, GDN_DV), lambda bi, n: (0, 0)),
                  pl.BlockSpec((1, SUBLANES, GDN_CONV_DIM), lambda bi, n: (bi, 0, 0)),
                  pl.BlockSpec((1, GDN_HEADS, GDN_DK, GDN_DV), lambda bi, n: (bi, 0, 0, 0))],
        out_specs=(pl.BlockSpec((1, c, GDN_WIDTH), lambda bi, n: (bi, n, 0)),
                   pl.BlockSpec((1, GDN_HEADS, GDN_DK, GDN_DV), lambda bi, n: (bi, 0, 0, 0))),
        scratch_shapes=[pltpu.VMEM((GDN_HEADS, GDN_DK, GDN_DV), F32), pltpu.VMEM((SUBLANES, GDN_CONV_DIM), F32)],
        compiler_params=_cparams("parallel", "arbitrary"),
        name="gdn",
    )(head_params, p3, p3, p3, p3, p3, conv_w, norm_g.reshape(1, GDN_DV), buf8, s0)


def _ret_kernel(q_ref, k_ref, v_ref, z_ref, ng_ref, s0_ref, o_ref, sout_ref, s_ref, *, t_true, c_true, past_len):
    n = pl.program_id(1)
    c = q_ref.shape[1]
    dk = q_ref.shape[2] // RET_HEADS
    dv = v_ref.shape[2] // RET_HEADS
    half = dk // 2

    @pl.when(n == 0)
    def _():
        s_ref[...] = s0_ref[0]

    rid = lax.broadcasted_iota(jnp.int32, (c, 1), 0)
    valid = ((rid + n * c) < t_true).astype(F32)
    pos = (rid + (n * c + past_len)).astype(F32)
    fidx = lax.broadcasted_iota(jnp.int32, (1, half), 1).astype(F32)
    inv = jnp.exp(fidx * (-math.log(ROPE_BASE) / half))
    ang = pos * inv
    cos, sin = jnp.cos(ang), jnp.sin(ang)
    ii = lax.broadcasted_iota(jnp.int32, (c, c), 0)
    jj = lax.broadcasted_iota(jnp.int32, (c, c), 1)
    diff = (ii - jj).astype(F32)
    idx = rid.astype(F32)

    def rot(x):
        x1, x2 = x[:, :half], x[:, half:]
        return jnp.concatenate([x1 * cos - x2 * sin, x2 * cos + x1 * sin], axis=-1)

    for h in range(RET_HEADS):
        lg = math.log1p(-(2.0 ** (-5.0 - h)))
        qh = rot(q_ref[0, :, h * dk:(h + 1) * dk]) * valid
        kh = rot(k_ref[0, :, h * dk:(h + 1) * dk]) * (valid * (dk ** -0.5))
        vh = v_ref[0, :, h * dv:(h + 1) * dv] * valid
        dmask = jnp.where(ii >= jj, jnp.exp(lg * jnp.maximum(diff, 0.0)), 0.0)
        xi = jnp.exp(lg * (idx + 1.0))
        zeta = jnp.exp(lg * (c_true - 1.0 - idx))
        s_old = s_ref[h]
        inner = _dot_nt(qh, kh) * dmask
        o = _dot(inner, vh) + _dot(qh * xi, s_old)
        s_ref[h] = math.exp(lg * c_true) * s_old + _dot_tn(kh * zeta, vh)
        o = _rms(o, ng_ref[...]) * _silu(z_ref[0, :, h * dv:(h + 1) * dv])
        o_ref[0, :, h * dv:(h + 1) * dv] = o

    @pl.when(n == pl.num_programs(1) - 1)
    def _():
        sout_ref[0] = s_ref[...]


def _retention(p3, norm_g, s0, t_true, past_len):
    b, tp, width = p3.shape
    d = width // 6
    dk, dv = d // RET_HEADS, 2 * d // RET_HEADS
    c = min(RET_CHUNK, tp)
    kern = functools.partial(_ret_kernel, t_true=t_true, c_true=min(RET_CHUNK, t_true), past_len=past_len)
    return pl.pallas_call(
        kern,
        out_shape=(jax.ShapeDtypeStruct((b, tp, 2 * d), F32),
                   jax.ShapeDtypeStruct((b, RET_HEADS, dk, dv), F32)),
        grid=(b, tp // c),
        in_specs=[pl.BlockSpec((1, c, d), lambda bi, n: (bi, n, 0)),
                  pl.BlockSpec((1, c, d), lambda bi, n: (bi, n, 1)),
                  pl.BlockSpec((1, c, 2 * d), lambda bi, n: (bi, n, 1)),
                  pl.BlockSpec((1, c, 2 * d), lambda bi, n: (bi, n, 2)),
                  pl.BlockSpec((1, dv), lambda bi, n: (0, 0)),
                  pl.BlockSpec((1, RET_HEADS, dk, dv), lambda bi, n: (bi, 0, 0, 0))],
        out_specs=(pl.BlockSpec((1, c, 2 * d), lambda bi, n: (bi, n, 0)),
                   pl.BlockSpec((1, RET_HEADS, dk, dv), lambda bi, n: (bi, 0, 0, 0))),
        scratch_shapes=[pltpu.VMEM((RET_HEADS, dk, dv), F32)],
        compiler_params=_cparams("parallel", "arbitrary"),
        name="retention",
    )(p3, p3, p3, p3, norm_g.reshape(1, dv), s0)


def _post_kernel(*refs, n_mix, final):
    x_ref = refs[0]
    mix_refs = refs[1:1 + n_mix]
    w_refs = refs[1 + n_mix:1 + 2 * n_mix]
    gxa_ref, wq_ref, mk_ref, mv_ref, wo_ref, gf_ref, o_ref = refs[1 + 2 * n_mix:]
    y = x_ref[0]
    for m_ref, w_ref in zip(mix_refs, w_refs):
        y = y + jnp.dot(m_ref[0].astype(BF16), w_ref[...], preferred_element_type=F32)
    h = _rms(y, gxa_ref[...])
    q = jnp.dot(h.astype(BF16), wq_ref[...], preferred_element_type=F32)
    d = q.shape[1]
    hd = d // XA_HEADS
    heads = []
    for i in range(XA_HEADS):
        sl = slice(i * hd, (i + 1) * hd)
        s = _dot_nt(q[:, sl], mk_ref[0, :, sl]) * (hd ** -0.5)
        s = s - jnp.max(s, axis=-1, keepdims=True)
        p = jnp.exp(s)
        p = p / jnp.sum(p, axis=-1, keepdims=True)
        heads.append(_dot(p, mv_ref[0, :, sl]))
    o = jnp.concatenate(heads, axis=-1)
    y = y + jnp.dot(o.astype(BF16), wo_ref[...], preferred_element_type=F32)
    if final:
        y = _rms(y, gf_ref[...])
    o_ref[0] = y


def _post(x, mixes, g_xa, wq, mem_k, mem_v, wo, g_final, final):
    b, tp, d = x.shape
    tm = min(tp, 256)
    mlen = mem_k.shape[1]
    row = lambda bi, i: (bi, i, 0)
    const = lambda bi, i: (0, 0)
    in_specs = [pl.BlockSpec((1, tm, d), row)]
    in_specs += [pl.BlockSpec((1, tm, a.shape[2]), row) for a, _ in mixes]
    in_specs += [pl.BlockSpec(w.shape, const) for _, w in mixes]
    in_specs += [pl.BlockSpec((1, d), const), pl.BlockSpec((d, d), const),
                 pl.BlockSpec((1, mlen, d), lambda bi, i: (bi, 0, 0)),
                 pl.BlockSpec((1, mlen, d), lambda bi, i: (bi, 0, 0)),
                 pl.BlockSpec((d, d), const), pl.BlockSpec((1, d), const)]
    return pl.pallas_call(
        functools.partial(_post_kernel, n_mix=len(mixes), final=final),
        out_shape=jax.ShapeDtypeStruct((b, tp, d), F32),
        grid=(b, tp // tm),
        in_specs=in_specs,
        out_specs=pl.BlockSpec((1, tm, d), row),
        compiler_params=_cparams("parallel", "arbitrary"),
        name="post",
    )(x, *[a for a, _ in mixes], *[w for _, w in mixes], g_xa.reshape(1, d), wq, mem_k, mem_v, wo,
      g_final.reshape(1, d))


def _prep_weights(w_in_ab, w_out_ab, w_in_c, w_out_c, xa_wq, xa_wo):
    n_ab_pad = -(-AB_IN // LANES) * LANES
    return dict(
        w_in_ab=jnp.pad(w_in_ab, ((0, 0), (0, 0), (0, n_ab_pad - AB_IN))).astype(BF16),
        w_out_ab=w_out_ab.astype(BF16), w_in_c=w_in_c.astype(BF16), w_out_c=w_out_c.astype(BF16),
        xa_wq=xa_wq.astype(BF16), xa_wo=xa_wo.astype(BF16))


def _trunk(x, past_len, sb_fn, conv_bufs, gdn_states, ret_states, mem_k, mem_v, wts,
           norm_mix_g, norm_xa_g, sb_logit_bias, conv_w_gdn, gdn_a_log, gdn_dt_bias, gdn_norm_g,
           ret_norm_g, norm_final_g):
    b, t, d = x.shape
    tp = -(-t // SUBLANES) * SUBLANES
    if tp != t:
        x = jnp.pad(x, ((0, 0), (0, tp - t), (0, 0)))
    depth = norm_mix_g.shape[0]
    mlen = mem_k.shape[2]
    ks, vs, convs, gdns, rets = [], [], [], [], []
    for layer in range(depth):
        i = layer // 2
        last = layer == depth - 1
        if layer % 2 == 0:
            p = _proj(x.reshape(b * tp, d), norm_mix_g[layer], wts["w_in_ab"][i], norm=True).reshape(b, tp, -1)
            o_sb = sb_fn(i, p)
            buf8 = jnp.pad(conv_bufs[i], ((0, 0), (SUBLANES - (CONV_W - 1), 0), (0, 0)))
            o_g, s_gdn = _gdn(p, conv_w_gdn[i], gdn_a_log[i], gdn_dt_bias[i], gdn_norm_g[i], buf8, gdn_states[i], t)
            ks.append(p[:, :t, SB_WIDTH:2 * SB_WIDTH].reshape(b, t, SB_HEADS, SB_HEAD_DIM))
            vs.append(p[:, :t, 2 * SB_WIDTH:3 * SB_WIDTH].reshape(b, t, SB_HEADS, SB_HEAD_DIM))
            qkv = p[:, :t, 4 * SB_WIDTH:4 * SB_WIDTH + GDN_CONV_DIM]
            keep = CONV_W - 1
            convs.append(jnp.concatenate([conv_bufs[i], qkv[:, max(t - keep, 0):]], axis=1)[:, -keep:])
            gdns.append(s_gdn)
            mixes = [(o_sb, wts["w_out_ab"][i, :SB_WIDTH]), (o_g, wts["w_out_ab"][i, SB_WIDTH:])]
        else:
            p = _proj(x.reshape(b * tp, d), norm_mix_g[layer], wts["w_in_c"][i], norm=True).reshape(b, tp, -1)
            o_r, s_ret = _retention(p, ret_norm_g[i], ret_states[i], t, past_len)
            rets.append(s_ret)
            mixes = [(o_r, wts["w_out_c"][i])]
        x = _post(x, mixes, norm_xa_g[layer], wts["xa_wq"][layer], mem_k[layer].reshape(b, mlen, d),
                  mem_v[layer].reshape(b, mlen, d), wts["xa_wo"][layer], norm_final_g, last)
    return (x[:, :t], jnp.stack(ks), jnp.stack(vs), jnp.stack(gdns), jnp.stack(convs), jnp.stack(rets))


def kernel(x_prompt, x_sample, mem_prompt, cache_sb_k, cache_sb_v, state_gdn, state_gdn_conv, state_ret,
           cache_mem_k, cache_mem_v, page_table, norm_mix_g, norm_xa_g, w_in_ab, sb_logit_bias, conv_w_gdn,
           gdn_a_log, gdn_dt_bias, gdn_norm_g, w_out_ab, w_in_c, ret_norm_g, w_out_c, xa_wq, xa_wk, xa_wv,
           xa_wo, norm_final_g):
    wts = _prep_weights(w_in_ab, w_out_ab, w_in_c, w_out_c, xa_wq, xa_wo)
    shared = (norm_mix_g, norm_xa_g, sb_logit_bias, conv_w_gdn, gdn_a_log, gdn_dt_bias, gdn_norm_g,
              ret_norm_g, norm_final_g)
    depth = norm_mix_g.shape[0]
    n_ab, n_c = state_gdn.shape[0], state_ret.shape[0]

    bp, mlen, d = mem_prompt.shape
    hd = d // XA_HEADS
    w_kv = jnp.concatenate([xa_wk, xa_wv], axis=0).astype(BF16)
    w_kv = jnp.moveaxis(w_kv, 0, 1).reshape(d, 2 * depth * d)
    mem_kv = _proj(mem_prompt.reshape(bp * mlen, d), jnp.ones((d,), F32), w_kv, norm=False)
    mem_kv = jnp.moveaxis(mem_kv.reshape(bp, mlen, 2 * depth, XA_HEADS, hd), 2, 0)
    mem_k_prompt, mem_v_prompt = mem_kv[:depth], mem_kv[depth:]
    conv0 = jnp.zeros((n_ab, bp) + state_gdn_conv.shape[2:], F32)
    gdn0 = jnp.zeros((n_ab, bp) + state_gdn.shape[2:], F32)
    ret0 = jnp.zeros((n_c, bp) + state_ret.shape[2:], F32)
    prompt = _trunk(x_prompt, 0, lambda i, p: _sb_prompt(p, sb_logit_bias[i]), conv0, gdn0, ret0,
                    mem_k_prompt, mem_v_prompt, wts, *shared)

    n_pages, page = page_table.shape[1], cache_sb_k.shape[2]
    assert x_sample.shape[1] == 1
    ck = cache_sb_k.reshape(cache_sb_k.shape[0], cache_sb_k.shape[1], page, SB_WIDTH)
    cv = cache_sb_v.reshape(cache_sb_v.shape[0], cache_sb_v.shape[1], page, SB_WIDTH)
    sample = _trunk(x_sample, n_pages * page,
                    lambda i, p: _sb_decode(p, sb_logit_bias[i], ck[i], cv[i], page_table),
                    state_gdn_conv, state_gdn, state_ret, cache_mem_k, cache_mem_v, wts, *shared)

    y_p, k_p, v_p, gdn_p, conv_p, ret_p = prompt
    y_s, k_s, v_s, gdn_s, conv_s, ret_s = sample
    return (y_p, y_s, k_p, v_p, gdn_p, conv_p, ret_p, mem_k_prompt, mem_v_prompt,
            k_s, v_s, gdn_s, conv_s, ret_s)
```

```python
import functools
import math

import numpy as np
import jax
import jax.numpy as jnp
from jax import lax
from jax.experimental import pallas as pl
from jax.experimental.pallas import tpu as pltpu

F32 = jnp.float32
BF16 = jnp.bfloat16

SB_HEADS = 8
SB_HEAD_DIM = 64
SB_WIDTH = SB_HEADS * SB_HEAD_DIM
SB_TILE = 128
GDN_HEADS = 4
GDN_DK = 128
GDN_DV = 128
GDN_WIDTH = GDN_HEADS * GDN_DV
GDN_CONV_DIM = GDN_HEADS * (2 * GDN_DK + GDN_DV)
CONV_W = 4
GDN_CHUNK = 64
AB_IN = 4 * SB_WIDTH + GDN_CONV_DIM + GDN_WIDTH + 2 * GDN_HEADS
RET_HEADS = 4
RET_CHUNK = 128
ROPE_BASE = 10000.0
XA_HEADS = 4
NORM_EPS = 1e-6
LANES = 128
SUBLANES = 8
VMEM_LIMIT = 56 * 1024 * 1024


def _cparams(*sem):
    return pltpu.CompilerParams(dimension_semantics=sem, vmem_limit_bytes=VMEM_LIMIT)


def _rms(x, g):
    ms = jnp.mean(x * x, axis=-1, keepdims=True)
    return x * lax.rsqrt(ms + NORM_EPS) * g


def _silu(x):
    return x * jax.nn.sigmoid(x)


def _softplus(x):
    return jnp.maximum(x, 0.0) + jnp.log1p(jnp.exp(-jnp.abs(x)))


def _dot(a, b):
    return jnp.dot(a.astype(BF16), b.astype(BF16), preferred_element_type=F32)


def _dot_nt(a, b):
    return lax.dot_general(a.astype(BF16), b.astype(BF16), (((1,), (1,)), ((), ())),
                           preferred_element_type=F32)


def _dot_tn(a, b):
    return lax.dot_general(a.astype(BF16), b.astype(BF16), (((0,), (0,)), ((), ())),
                           preferred_element_type=F32)


def _split_bf16(x):
    hi = x.astype(BF16)
    lo = (x - hi.astype(F32)).astype(BF16)
    return hi, lo


def _proj_kernel(x_ref, g_ref, w_ref, o_ref, *rest, norm, emit_bf16):
    h_ref = rest[-1]

    @pl.when(pl.program_id(1) == 0)
    def _():
        x = x_ref[...]
        if norm:
            x = _rms(x, g_ref[...])
        h_ref[...] = x.astype(BF16)

    res = jnp.dot(h_ref[...], w_ref[...], preferred_element_type=F32)
    o_ref[...] = res
    if emit_bf16:
        rest[0][...] = res.astype(BF16)


def _pick_tile(n, cap, unit):
    best = unit
    for t in range(unit, min(n, cap) + 1, unit):
        if n % t == 0:
            best = t
    return best


def _proj(x2d, g, w_bf16, *, norm, emit_bf16=False):
    m, d = x2d.shape
    n = w_bf16.shape[1]
    tm = m if m <= 512 else _pick_tile(m, 512, SUBLANES)
    tn = _pick_tile(n, 2048, LANES)
    out_spec = pl.BlockSpec((tm, tn), lambda i, j: (i, j))
    out_shape = jax.ShapeDtypeStruct((m, n), F32)
    if emit_bf16:
        out_shape = (out_shape, jax.ShapeDtypeStruct((m, n), BF16))
        out_spec = (out_spec, pl.BlockSpec((tm, tn), lambda i, j: (i, j)))
    return pl.pallas_call(
        functools.partial(_proj_kernel, norm=norm, emit_bf16=emit_bf16),
        out_shape=out_shape,
        grid=(m // tm, n // tn),
        in_specs=[pl.BlockSpec((tm, d), lambda i, j: (i, 0)),
                  pl.BlockSpec((1, d), lambda i, j: (0, 0)),
                  pl.BlockSpec((d, tn), lambda i, j: (0, j))],
        out_specs=out_spec,
        scratch_shapes=[pltpu.VMEM((tm, d), BF16)],
        compiler_params=_cparams("parallel", "arbitrary"),
        name="proj",
    )(x2d, g.reshape(1, d), w_bf16)


def _suffix_matrix(n):
    j = np.arange(n)[:, None]
    s = np.arange(n)[None, :]
    return jnp.asarray((j >= s).astype(np.float32), dtype=BF16)


def _log_keep(z):
    return -(jnp.maximum(z, 0.0) + jnp.log(1.0 + jnp.exp(-jnp.abs(z))))


LOG2E = math.log2(math.e)


def _sb_prompt_kernel(bias_ref, q_ref, k_ref, v_ref, z_ref, tri_ref, o_ref, c_ref, acc_ref, qm_ref):
    i = pl.program_id(1)
    tq = q_ref.shape[1]
    tk = tq
    pairs = SB_HEADS // 2
    lane = lax.broadcasted_iota(jnp.int32, (tq, LANES), 1)
    first = lane < SB_HEAD_DIM
    row = lax.broadcasted_iota(jnp.int32, (2 * tq, tk), 0)
    col = lax.broadcasted_iota(jnp.int32, (2 * tq, tk), 1)
    causal = col < jnp.where(row >= tq, row - tq, row)
    tri = tri_ref[...]
    c_ref[...] = jnp.zeros_like(c_ref)
    acc_ref[...] = jnp.zeros_like(acc_ref)
    for p in range(pairs):
        qp = q_ref[0, :, p * LANES:(p + 1) * LANES] * (LOG2E * SB_HEAD_DIM ** -0.5)
        qm_ref[p, :tq] = jnp.where(first, qp, 0.0).astype(BF16)
        qm_ref[p, tq:] = jnp.where(first, 0.0, qp).astype(BF16)

    def tile(j, masked):
        start = pl.multiple_of(j * tk, tk)
        z2s, sps = [], []
        for p in range(pairs):
            kb = k_ref[0, pl.ds(start, tk), p * LANES:(p + 1) * LANES]
            s = lax.dot_general(qm_ref[p], kb, (((1,), (1,)), ((), ())), preferred_element_type=F32)
            z2 = jnp.concatenate([s[:tq] + bias_ref[2 * p] * LOG2E, s[tq:] + bias_ref[2 * p + 1] * LOG2E], axis=0)
            neg_abs = lax.bitcast_convert_type(
                lax.bitcast_convert_type(z2, jnp.uint32) | jnp.uint32(0x80000000), F32)
            sp = jnp.maximum(z2, 0.0) + jnp.log2(1.0 + jnp.exp2(neg_abs))
            if masked:
                sp = jnp.where(causal, sp, 0.0)
            z2s.append(z2)
            sps.append(sp)
        incl_all = jnp.dot(jnp.concatenate(sps, axis=0).astype(BF16), tri, preferred_element_type=F32)
        for p in range(pairs):
            vb = v_ref[0, pl.ds(start, tk), p * LANES:(p + 1) * LANES]
            incl = incl_all[p * 2 * tq:(p + 1) * 2 * tq]
            c = c_ref[p]
            a = jnp.exp2(z2s[p] - incl - jnp.concatenate([c] * (tk // LANES), axis=1))
            if masked:
                a = jnp.where(causal, a, 0.0)
            acc_ref[p] += jnp.dot(a.astype(BF16), vb, preferred_element_type=F32)
            c_ref[p] = c + jnp.broadcast_to(incl[:, 0:1], (2 * tq, LANES))

    tile(i, True)

    def body(jj, carry):
        tile(i - 1 - jj, False)
        return carry

    lax.fori_loop(0, i, body, 0)
    for p in range(pairs):
        cols = slice(p * LANES, (p + 1) * LANES)
        o = jnp.where(first, acc_ref[p, :tq], acc_ref[p, tq:])
        o_ref[0, :, cols] = o * _silu(z_ref[0, :, cols])


SB_PROMPT_TILE = 256


def _sb_prompt(p3, pb3, bias):
    b, t, _ = p3.shape
    tq = min(SB_PROMPT_TILE, t)
    assert t % tq == 0 and tq % LANES == 0
    return pl.pallas_call(
        _sb_prompt_kernel,
        out_shape=jax.ShapeDtypeStruct((b, t, SB_WIDTH), F32),
        grid=(b, t // tq),
        in_specs=[pl.BlockSpec(memory_space=pltpu.SMEM),
                  pl.BlockSpec((1, tq, SB_WIDTH), lambda bi, i: (bi, i, 0)),
                  pl.BlockSpec((1, t, SB_WIDTH), lambda bi, i: (bi, 0, 1)),
                  pl.BlockSpec((1, t, SB_WIDTH), lambda bi, i: (bi, 0, 2)),
                  pl.BlockSpec((1, tq, SB_WIDTH), lambda bi, i: (bi, i, 3)),
                  pl.BlockSpec((tq, tq), lambda bi, i: (0, 0))],
        out_specs=pl.BlockSpec((1, tq, SB_WIDTH), lambda bi, i: (bi, i, 0)),
        scratch_shapes=[pltpu.VMEM((SB_HEADS // 2, 2 * tq, LANES), F32),
                        pltpu.VMEM((SB_HEADS // 2, 2 * tq, LANES), F32),
                        pltpu.VMEM((SB_HEADS // 2, 2 * tq, LANES), BF16)],
        compiler_params=_cparams("parallel", "arbitrary"),
        name="sb_prompt",
    )(bias, p3, pb3, pb3, p3, _suffix_matrix(tq))


SB_PAGES_PER_STEP = 8


def _tree_sum(xs):
    while len(xs) > 1:
        xs = [xs[k] + xs[k + 1] for k in range(0, len(xs) - 1, 2)] + ([xs[-1]] if len(xs) % 2 else [])
    return xs[0]


def _fold_sublanes(tiles):
    assert len(tiles) == SUBLANES
    order = [0, 4, 2, 6, 1, 5, 3, 7]
    cur = [tiles[k] for k in order]
    sub = lax.broadcasted_iota(jnp.int32, cur[0].shape, 0)
    s = SUBLANES // 2
    while s >= 1:
        low = (sub // s) % 2 == 0
        nxt = []
        for k in range(0, len(cur), 2):
            a, b = cur[k], cur[k + 1]
            fa = a + pltpu.roll(a, SUBLANES - s, axis=0)
            fb = b + pltpu.roll(b, s, axis=0)
            nxt.append(jnp.where(low, fa, fb))
        cur = nxt
        s //= 2
    return cur[0]


def _sb_decode_kernel(pt_ref, qb_ref, z_ref, bias_ref, tri_ref, *rest):
    del pt_ref
    g = SB_PAGES_PER_STEP
    k_refs = rest[:g]
    v_refs = rest[g:2 * g]
    o_ref = rest[2 * g]
    c_ref, acc_ref, a_ref, qs_ref = rest[2 * g + 1:]
    step = pl.program_id(1)
    page = k_refs[0].shape[3]

    @pl.when(step == 0)
    def _():
        c_ref[...] = jnp.zeros_like(c_ref)
        acc_ref[...] = jnp.zeros_like(acc_ref)
        qs_ref[...] = qb_ref[0] * (SB_HEAD_DIM ** -0.5)

    parts = [[None] * SB_HEADS for _ in range(g)]
    for h in range(SB_HEADS):
        rows = slice(h * SB_HEAD_DIM, (h + 1) * SB_HEAD_DIM)
        qh = qs_ref[rows, :]
        for i in range(g):
            prod = k_refs[i][0, 0, rows, :] * qh
            parts[i][h] = _tree_sum([prod[r:r + SUBLANES] for r in range(0, SB_HEAD_DIM, SUBLANES)])
    bias = bias_ref[...]
    zs = [_fold_sublanes(parts[i]) + bias for i in range(g)]
    lk = jnp.concatenate([_log_keep(z) for z in zs], axis=0)
    hi, lo = _split_bf16(lk)
    tri = tri_ref[...]
    r = jnp.dot(hi, tri, preferred_element_type=F32) + jnp.dot(lo, tri, preferred_element_type=F32)
    c = c_ref[...]
    for i in range(g):
        rows = slice(i * SB_HEADS, (i + 1) * SB_HEADS)
        a_ref[i] = jnp.exp(zs[i] + r[rows, :page] + c)
        c = c + r[rows, page:]
    c_ref[...] = c
    for h in range(SB_HEADS):
        rows = slice(h * SB_HEAD_DIM, (h + 1) * SB_HEAD_DIM)
        acc = acc_ref[rows, :]
        for i in range(g):
            acc = acc + v_refs[i][0, 0, rows, :] * jnp.broadcast_to(a_ref[i, h:h + 1, :], (SB_HEAD_DIM, page))
        acc_ref[rows, :] = acc

    @pl.when(step == pl.num_programs(1) - 1)
    def _():
        hi, lo = _split_bf16(acc_ref[...])
        ones = jnp.ones((SUBLANES, page), BF16)
        nt = (((1,), (1,)), ((), ()))
        o = (lax.dot_general(ones, hi, nt, preferred_element_type=F32)
             + lax.dot_general(ones, lo, nt, preferred_element_type=F32))
        rid = lax.broadcasted_iota(jnp.int32, o.shape, 0)
        o_ref[0] = jnp.where(rid == 0, o, 0.0) * _silu(z_ref[0])


def _sb_decode(p3, bias, cache_kt, cache_vt, layer, page_table):
    b, rows, _ = p3.shape
    n_pages = page_table.shape[1]
    page = cache_kt.shape[3]
    g = SB_PAGES_PER_STEP
    assert n_pages % g == 0 and page == LANES and rows == SUBLANES
    steps = n_pages // g

    def page_spec(i):
        return pl.BlockSpec((1, 1, SB_WIDTH, page),
                            lambda bi, s, pt: (layer, pt[bi, n_pages - 1 - (s * g + i)], 0, 0))

    bias_b = jnp.broadcast_to(bias.astype(F32)[:, None], (SB_HEADS, page))
    tri = jnp.concatenate([_suffix_matrix(page), jnp.ones((page, page), BF16)], axis=1)
    q_b =jnp.broadcast_to(p3[:, 0, :SB_WIDTH, None], (b, SB_WIDTH, page))
    grid_spec = pltpu.PrefetchScalarGridSpec(
        num_scalar_prefetch=1,
        grid=(b, steps),
        in_specs=[pl.BlockSpec((1, SB_WIDTH, page), lambda bi, s, pt: (bi, 0, 0)),
                  pl.BlockSpec((1, rows, SB_WIDTH), lambda bi, s, pt: (bi, 0, 3)),
                  pl.BlockSpec((SB_HEADS, page), lambda bi, s, pt: (0, 0)),
                  pl.BlockSpec((page, 2 * page), lambda bi, s, pt: (0, 0))]
        + [page_spec(i) for i in range(g)] + [page_spec(i) for i in range(g)],
        out_specs=pl.BlockSpec((1, rows, SB_WIDTH), lambda bi, s, pt: (bi, 0, 0)),
        scratch_shapes=[pltpu.VMEM((SB_HEADS, page), F32), pltpu.VMEM((SB_WIDTH, page), F32),
                        pltpu.VMEM((g, SB_HEADS, page), F32), pltpu.VMEM((SB_WIDTH, page), F32)],
    )
    return pl.pallas_call(
        _sb_decode_kernel,
        out_shape=jax.ShapeDtypeStruct((b, rows, SB_WIDTH), F32),
        grid_spec=grid_spec,
        compiler_params=_cparams("parallel", "arbitrary"),
        name="sb_decode",
    )(page_table, q_b, p3, bias_b, tri, *([cache_kt] * g), *([cache_vt] * g))


def _gdn_kernel(hp_ref, q_ref, k_ref, v_ref, gz_ref, gate_ref, cw_ref, ng_ref, buf_ref, s0_ref,
                o_ref, sout_ref, s_ref, tail_ref, *, t_true):
    n = pl.program_id(1)
    c = q_ref.shape[1]
    hw = GDN_HEADS * GDN_DK

    @pl.when(n == 0)
    def _():
        s_ref[...] = s0_ref[...]
        tail_ref[...] = buf_ref[...]

    rowid = lax.broadcasted_iota(jnp.int32, (c, 1), 0) + n * c
    valid = (rowid < t_true).astype(F32)
    cw = cw_ref[...]
    ii = lax.broadcasted_iota(jnp.int32, (c, c), 0)
    jj = lax.broadcasted_iota(jnp.int32, (c, c), 1)
    lower_incl = (jj <= ii).astype(BF16)
    upper_incl = ii <= jj
    ones8 = jnp.ones((SUBLANES, c), BF16)
    eye = (ii == jj).astype(F32)

    for bi in range(q_ref.shape[0]):
        _gdn_chunk(bi, hp_ref, q_ref, k_ref, v_ref, gz_ref, gate_ref, ng_ref, o_ref, s_ref, tail_ref,
                   cw, valid, ii, jj, lower_incl, upper_incl, ones8, eye)

    @pl.when(n == pl.num_programs(1) - 1)
    def _():
        sout_ref[...] = s_ref[...]


def _gdn_chunk(bi, hp_ref, q_ref, k_ref, v_ref, gz_ref, gate_ref, ng_ref, o_ref, s_ref, tail_ref,
               cw, valid, ii, jj, lower_incl, upper_incl, ones8, eye):
    c = q_ref.shape[1]
    hw = GDN_HEADS * GDN_DK
    tail = tail_ref[bi]

    def conv(x_ref, sec):
        x = x_ref[bi]
        xx = jnp.concatenate([tail[:, sec * hw:(sec + 1) * hw], x], axis=0)
        w = cw[:, sec * hw:(sec + 1) * hw]
        y = w[CONV_W - 1:CONV_W] * x
        for i in range(CONV_W - 1):
            off = SUBLANES - (CONV_W - 1) + i
            y = y + w[i:i + 1] * xx[off:off + c]
        tail_ref[bi, :, sec * hw:(sec + 1) * hw] = x[c - SUBLANES:, :]
        return _silu(y) * valid

    yq, yk, yv = conv(q_ref, 0), conv(k_ref, 1), conv(v_ref, 2)
    gates = gate_ref[bi]
    gz = gz_ref[bi]

    for h in range(GDN_HEADS):
        sl = slice(h * GDN_DK, (h + 1) * GDN_DK)
        qh, kh, vh = yq[:, sl], yk[:, sl], yv[:, sl]
        qh = qh * lax.rsqrt(jnp.sum(qh * qh, axis=-1, keepdims=True) + NORM_EPS) * (GDN_DK ** -0.5)
        kh = kh * lax.rsqrt(jnp.sum(kh * kh, axis=-1, keepdims=True) + NORM_EPS)
        beta = jax.nn.sigmoid(gates[:, GDN_HEADS + h:GDN_HEADS + h + 1]) * valid
        a_log = hp_ref[0:1, h:h + 1]
        dt_bias = hp_ref[1:2, h:h + 1]
        gl = -jnp.exp(a_log) * _softplus(gates[:, h:h + 1] + dt_bias) * valid
        gb = jnp.broadcast_to(gl, (c, c))
        g_hi, g_lo = _split_bf16(gb)
        gcol = (jnp.dot(lower_incl, g_hi, preferred_element_type=F32)
                + jnp.dot(lower_incl, g_lo, preferred_element_type=F32))
        u_hi, u_lo = _split_bf16(jnp.where(upper_incl, gb, 0.0))
        grow = (jnp.dot(ones8, u_hi, preferred_element_type=F32)
                + jnp.dot(ones8, u_lo, preferred_element_type=F32))[0:1, :]
        decay = jnp.where(ii >= jj, jnp.exp(gcol - grow), 0.0)
        gc = gcol[:, 0:1]
        g_last = gcol[c - 1:c, 0:1]
        e_gc = jnp.exp(gc)
        kb = kh * beta
        m = jnp.where(ii > jj, _dot_nt(kb, kh) * decay, 0.0)
        tinv = eye - m
        pw = m
        k = 2
        while k < c:
            pw = _dot(pw, pw)
            tinv = tinv + _dot(tinv, pw)
            k *= 2
        u = _dot(tinv, vh * beta)
        w = _dot(tinv, kb * e_gc)
        s_old = s_ref[bi, h]
        s_b = s_old.astype(BF16)
        v_new = u - _dot(w, s_b)
        attn = _dot_nt(qh, kh) * decay
        o = _dot(qh * e_gc, s_b) + _dot(attn, v_new)
        s_ref[bi, h] = s_old * jnp.exp(g_last) + _dot_tn(kh * jnp.exp(g_last - gc), v_new)
        o = _rms(o, ng_ref[...]) * _silu(gz[:, h * GDN_DV:(h + 1) * GDN_DV])
        o_ref[bi, :, h * GDN_DV:(h + 1) * GDN_DV] = o


GDN_SEQS_PER_STEP = 4


def _gdn(p3, conv_w, a_log, dt_bias, norm_g, buf8, s0, t_true):
    b, tp, _ = p3.shape
    c = min(GDN_CHUNK, tp)
    bb = math.gcd(b, GDN_SEQS_PER_STEP)
    hw = GDN_HEADS * GDN_DK
    base = 4 * SB_WIDTH // hw
    kern = functools.partial(_gdn_kernel, t_true=t_true)
    head_params = jnp.pad(jnp.stack([a_log, dt_bias]).astype(F32), ((0, 0), (0, LANES - GDN_HEADS)))
    return pl.pallas_call(
        kern,
        out_shape=(jax.ShapeDtypeStruct((b, tp, GDN_WIDTH), F32),
                   jax.ShapeDtypeStruct((b, GDN_HEADS, GDN_DK, GDN_DV), F32)),
        grid=(b // bb, tp // c),
        in_specs=[pl.BlockSpec((2, LANES), lambda bi, n: (0, 0)),
                  pl.BlockSpec((bb, c, hw), lambda bi, n: (bi, n, base)),
                  pl.BlockSpec((bb, c, hw), lambda bi, n: (bi, n, base + 1)),
                  pl.BlockSpec((bb, c, hw), lambda bi, n: (bi, n, base + 2)),
                  pl.BlockSpec((bb, c, GDN_WIDTH), lambda bi, n: (bi, n, base + 3)),
                  pl.BlockSpec((bb, c, LANES), lambda bi, n: (bi, n, (4 * SB_WIDTH + GDN_CONV_DIM + GDN_WIDTH) // LANES)),
                  pl.BlockSpec((CONV_W, GDN_CONV_DIM), lambda bi, n: (0, 0)),
                  pl.BlockSpec((1, GDN_DV), lambda bi, n: (0, 0)),
                  pl.BlockSpec((bb, SUBLANES, GDN_CONV_DIM), lambda bi, n: (bi, 0, 0)),
                  pl.BlockSpec((bb, GDN_HEADS, GDN_DK, GDN_DV), lambda bi, n: (bi, 0, 0, 0))],
        out_specs=(pl.BlockSpec((bb, c, GDN_WIDTH), lambda bi, n: (bi, n, 0)),
                   pl.BlockSpec((bb, GDN_HEADS, GDN_DK, GDN_DV), lambda bi, n: (bi, 0, 0, 0))),
        scratch_shapes=[pltpu.VMEM((bb, GDN_HEADS, GDN_DK, GDN_DV), F32),
                        pltpu.VMEM((bb, SUBLANES, GDN_CONV_DIM), F32)],
        compiler_params=_cparams("parallel", "arbitrary"),
        name="gdn",
    )(head_params, p3, p3, p3, p3, p3, conv_w, norm_g.reshape(1, GDN_DV), buf8, s0)


def _ret_kernel(q_ref, k_ref, v_ref, z_ref, ng_ref, s0_ref, o_ref, sout_ref, s_ref, *, t_true, c_true, past_len):
    n = pl.program_id(1)
    c = q_ref.shape[1]
    dk = q_ref.shape[2] // RET_HEADS
    dv = v_ref.shape[2] // RET_HEADS
    half = dk // 2

    @pl.when(n == 0)
    def _():
        s_ref[...] = s0_ref[0]

    rid = lax.broadcasted_iota(jnp.int32, (c, 1), 0)
    valid = ((rid + n * c) < t_true).astype(F32)
    pos = (rid + (n * c + past_len)).astype(F32)
    fidx = lax.broadcasted_iota(jnp.int32, (1, half), 1).astype(F32)
    inv = jnp.exp(fidx * (-math.log(ROPE_BASE) / half))
    ang = pos * inv
    cos, sin = jnp.cos(ang), jnp.sin(ang)
    ii = lax.broadcasted_iota(jnp.int32, (c, c), 0)
    jj = lax.broadcasted_iota(jnp.int32, (c, c), 1)
    diff = (ii - jj).astype(F32)
    idx = rid.astype(F32)

    def rot(x):
        x1, x2 = x[:, :half], x[:, half:]
        return jnp.concatenate([x1 * cos - x2 * sin, x2 * cos + x1 * sin], axis=-1)

    for h in range(RET_HEADS):
        lg = math.log1p(-(2.0 ** (-5.0 - h)))
        qh = rot(q_ref[0, :, h * dk:(h + 1) * dk]) * valid
        kh = rot(k_ref[0, :, h * dk:(h + 1) * dk]) * (valid * (dk ** -0.5))
        vh = v_ref[0, :, h * dv:(h + 1) * dv] * valid
        dmask = jnp.where(ii >= jj, jnp.exp(lg * jnp.maximum(diff, 0.0)), 0.0)
        xi = jnp.exp(lg * (idx + 1.0))
        zeta = jnp.exp(lg * (c_true - 1.0 - idx))
        s_old = s_ref[h]
        inner = _dot_nt(qh, kh) * dmask
        o = _dot(inner, vh) + _dot(qh * xi, s_old)
        s_ref[h] = math.exp(lg * c_true) * s_old + _dot_tn(kh * zeta, vh)
        o = _rms(o, ng_ref[...]) * _silu(z_ref[0, :, h * dv:(h + 1) * dv])
        o_ref[0, :, h * dv:(h + 1) * dv] = o

    @pl.when(n == pl.num_programs(1) - 1)
    def _():
        sout_ref[0] = s_ref[...]


def _retention(p3, norm_g, s0, t_true, past_len):
    b, tp, width = p3.shape
    d = width // 6
    dk, dv = d // RET_HEADS, 2 * d // RET_HEADS
    c = min(RET_CHUNK, tp)
    kern = functools.partial(_ret_kernel, t_true=t_true, c_true=min(RET_CHUNK, t_true), past_len=past_len)
    return pl.pallas_call(
        kern,
        out_shape=(jax.ShapeDtypeStruct((b, tp, 2 * d), F32),
                   jax.ShapeDtypeStruct((b, RET_HEADS, dk, dv), F32)),
        grid=(b, tp // c),
        in_specs=[pl.BlockSpec((1, c, d), lambda bi, n: (bi, n, 0)),
                  pl.BlockSpec((1, c, d), lambda bi, n: (bi, n, 1)),
                  pl.BlockSpec((1, c, 2 * d), lambda bi, n: (bi, n, 1)),
                  pl.BlockSpec((1, c, 2 * d), lambda bi, n: (bi, n, 2)),
                  pl.BlockSpec((1, dv), lambda bi, n: (0, 0)),
                  pl.BlockSpec((1, RET_HEADS, dk, dv), lambda bi, n: (bi, 0, 0, 0))],
        out_specs=(pl.BlockSpec((1, c, 2 * d), lambda bi, n: (bi, n, 0)),
                   pl.BlockSpec((1, RET_HEADS, dk, dv), lambda bi, n: (bi, 0, 0, 0))),
        scratch_shapes=[pltpu.VMEM((RET_HEADS, dk, dv), F32)],
        compiler_params=_cparams("parallel", "arbitrary"),
        name="retention",
    )(p3, p3, p3, p3, norm_g.reshape(1, dv), s0)


def _post_kernel(*refs, n_mix, final):
    x_ref = refs[0]
    mix_refs = refs[1:1 + n_mix]
    w_refs = refs[1 + n_mix:1 + 2 * n_mix]
    gxa_ref, wq_ref, mk_ref, mv_ref, wo_ref, gf_ref, o_ref = refs[1 + 2 * n_mix:]
    y = x_ref[0]
    for m_ref, w_ref in zip(mix_refs, w_refs):
        y = y + jnp.dot(m_ref[0].astype(BF16), w_ref[...], preferred_element_type=F32)
    h = _rms(y, gxa_ref[...])
    q = jnp.dot(h.astype(BF16), wq_ref[...], preferred_element_type=F32)
    d = q.shape[1]
    hd = d // XA_HEADS
    heads = []
    for i in range(XA_HEADS):
        sl = slice(i * hd, (i + 1) * hd)
        s = _dot_nt(q[:, sl], mk_ref[0, i]) * (hd ** -0.5)
        s = s - jnp.max(s, axis=-1, keepdims=True)
        p = jnp.exp(s)
        p = p / jnp.sum(p, axis=-1, keepdims=True)
        heads.append(_dot(p, mv_ref[0, i]))
    o = jnp.concatenate(heads, axis=-1)
    y = y + jnp.dot(o.astype(BF16), wo_ref[...], preferred_element_type=F32)
    if final:
        y = _rms(y, gf_ref[...])
    o_ref[0] = y


def _post(x, mixes, g_xa, wq, mem_k, mem_v, wo, g_final, final):
    b, tp, d = x.shape
    tm = min(tp, 256)
    mem_block = (1,) + mem_k.shape[1:]
    row = lambda bi, i: (bi, i, 0)
    const = lambda bi, i: (0, 0)
    in_specs = [pl.BlockSpec((1, tm, d), row)]
    in_specs += [pl.BlockSpec((1, tm, a.shape[2]), row) for a, _ in mixes]
    in_specs += [pl.BlockSpec(w.shape, const) for _, w in mixes]
    in_specs += [pl.BlockSpec((1, d), const), pl.BlockSpec((d, d), const),
                 pl.BlockSpec(mem_block, lambda bi, i: (bi, 0, 0, 0)),
                 pl.BlockSpec(mem_block, lambda bi, i: (bi, 0, 0, 0)),
                 pl.BlockSpec((d, d), const), pl.BlockSpec((1, d), const)]
    return pl.pallas_call(
        functools.partial(_post_kernel, n_mix=len(mixes), final=final),
        out_shape=jax.ShapeDtypeStruct((b, tp, d), F32),
        grid=(b, tp // tm),
        in_specs=in_specs,
        out_specs=pl.BlockSpec((1, tm, d), row),
        compiler_params=_cparams("parallel", "arbitrary"),
        name="post",
    )(x, *[a for a, _ in mixes], *[w for _, w in mixes], g_xa.reshape(1, d), wq, mem_k, mem_v, wo,
      g_final.reshape(1, d))


def _prep_weights(w_in_ab, w_out_ab, w_in_c, w_out_c, xa_wq, xa_wo):
    n_ab_pad = -(-AB_IN // LANES) * LANES
    return dict(
        w_in_ab=jnp.pad(w_in_ab, ((0, 0), (0, 0), (0, n_ab_pad - AB_IN))).astype(BF16),
        w_out_ab=w_out_ab.astype(BF16), w_in_c=w_in_c.astype(BF16), w_out_c=w_out_c.astype(BF16),
        xa_wq=xa_wq.astype(BF16), xa_wo=xa_wo.astype(BF16))


def _trunk(x, past_len, sb_fn, conv_bufs, gdn_states, ret_states, mem_k, mem_v, wts,
           norm_mix_g, norm_xa_g, sb_logit_bias, conv_w_gdn, gdn_a_log, gdn_dt_bias, gdn_norm_g,
           ret_norm_g, norm_final_g):
    b, t, d = x.shape
    tp = -(-t // SUBLANES) * SUBLANES
    if tp != t:
        x = jnp.pad(x, ((0, 0), (0, tp - t), (0, 0)))
    depth = norm_mix_g.shape[0]
    ks, vs, convs, gdns, rets = [], [], [], [], []
    for layer in range(depth):
        i = layer // 2
        last = layer == depth - 1
        if layer % 2 == 0:
            p, pb = _proj(x.reshape(b * tp, d), norm_mix_g[layer], wts["w_in_ab"][i], norm=True, emit_bf16=True)
            p = p.reshape(b, tp, -1)
            o_sb = sb_fn(i, p, pb.reshape(b, tp, -1))
            buf8 = jnp.pad(conv_bufs[i], ((0, 0), (SUBLANES - (CONV_W - 1), 0), (0, 0)))
            o_g, s_gdn = _gdn(p, conv_w_gdn[i], gdn_a_log[i], gdn_dt_bias[i], gdn_norm_g[i], buf8, gdn_states[i], t)
            ks.append(p[:, :t, SB_WIDTH:2 * SB_WIDTH].reshape(b, t, SB_HEADS, SB_HEAD_DIM))
            vs.append(p[:, :t, 2 * SB_WIDTH:3 * SB_WIDTH].reshape(b, t, SB_HEADS, SB_HEAD_DIM))
            qkv = p[:, :t, 4 * SB_WIDTH:4 * SB_WIDTH + GDN_CONV_DIM]
            keep = CONV_W - 1
            convs.append(jnp.concatenate([conv_bufs[i], qkv[:, max(t - keep, 0):]], axis=1)[:, -keep:])
            gdns.append(s_gdn)
            mixes = [(o_sb, wts["w_out_ab"][i, :SB_WIDTH]), (o_g, wts["w_out_ab"][i, SB_WIDTH:])]
        else:
            p = _proj(x.reshape(b * tp, d), norm_mix_g[layer], wts["w_in_c"][i], norm=True).reshape(b, tp, -1)
            o_r, s_ret = _retention(p, ret_norm_g[i], ret_states[i], t, past_len)
            rets.append(s_ret)
            mixes = [(o_r, wts["w_out_c"][i])]
        x = _post(x, mixes, norm_xa_g[layer], wts["xa_wq"][layer], mem_k[layer], mem_v[layer],
                  wts["xa_wo"][layer], norm_final_g, last)
    return (x[:, :t], jnp.stack(ks), jnp.stack(vs), jnp.stack(gdns), jnp.stack(convs), jnp.stack(rets))


def kernel(x_prompt, x_sample, mem_prompt, cache_sb_k, cache_sb_v, state_gdn, state_gdn_conv, state_ret,
           cache_mem_k, cache_mem_v, page_table, norm_mix_g, norm_xa_g, w_in_ab, sb_logit_bias, conv_w_gdn,
           gdn_a_log, gdn_dt_bias, gdn_norm_g, w_out_ab, w_in_c, ret_norm_g, w_out_c, xa_wq, xa_wk, xa_wv,
           xa_wo, norm_final_g):
    wts = _prep_weights(w_in_ab, w_out_ab, w_in_c, w_out_c, xa_wq, xa_wo)
    shared = (norm_mix_g, norm_xa_g, sb_logit_bias, conv_w_gdn, gdn_a_log, gdn_dt_bias, gdn_norm_g,
              ret_norm_g, norm_final_g)
    depth = norm_mix_g.shape[0]
    n_ab, n_c = state_gdn.shape[0], state_ret.shape[0]

    bp, mlen, d = mem_prompt.shape
    hd = d // XA_HEADS
    w_kv = jnp.concatenate([xa_wk, xa_wv], axis=0).astype(BF16)
    w_kv = jnp.moveaxis(w_kv, 0, 1).reshape(d, 2 * depth * d)
    mem_kv = _proj(mem_prompt.reshape(bp * mlen, d), jnp.ones((d,), F32), w_kv, norm=False)
    mem_kv = mem_kv.reshape(bp, mlen, 2 * depth, XA_HEADS, hd)
    mem_hm = jnp.transpose(mem_kv, (2, 0, 3, 1, 4))
    mem_kv = jnp.moveaxis(mem_kv, 2, 0)
    mem_k_prompt, mem_v_prompt = mem_kv[:depth], mem_kv[depth:]
    conv0 = jnp.zeros((n_ab, bp) + state_gdn_conv.shape[2:], F32)
    gdn0 = jnp.zeros((n_ab, bp) + state_gdn.shape[2:], F32)
    ret0 = jnp.zeros((n_c, bp) + state_ret.shape[2:], F32)
    prompt = _trunk(x_prompt, 0, lambda i, p, pb: _sb_prompt(p, pb, sb_logit_bias[i]), conv0, gdn0, ret0,
                    mem_hm[:depth], mem_hm[depth:], wts, *shared)

    n_pages, page = page_table.shape[1], cache_sb_k.shape[2]
    assert x_sample.shape[1] == 1
    n_phys = cache_sb_k.shape[1]
    ckt = jnp.transpose(cache_sb_k, (0, 1, 3, 4, 2)).reshape(n_ab, n_phys, SB_WIDTH, page)
    cvt = jnp.transpose(cache_sb_v, (0, 1, 3, 4, 2)).reshape(n_ab, n_phys, SB_WIDTH, page)
    smem_k = jnp.transpose(cache_mem_k, (0, 1, 3, 2, 4))
    smem_v = jnp.transpose(cache_mem_v, (0, 1, 3, 2, 4))
    sample = _trunk(x_sample, n_pages * page,
                    lambda i, p, pb: _sb_decode(p, sb_logit_bias[i], ckt, cvt, i, page_table),
                    state_gdn_conv, state_gdn, state_ret, smem_k, smem_v, wts, *shared)

    y_p, k_p, v_p, gdn_p, conv_p, ret_p = prompt
    y_s, k_s, v_s, gdn_s, conv_s, ret_s = sample
    return (y_p, y_s, k_p, v_p, gdn_p, conv_p, ret_p, mem_k_prompt, mem_v_prompt,
            k_s, v_s, gdn_s, conv_s, ret_s)
```

```python
import functools
import math

import numpy as np
import jax
import jax.numpy as jnp
from jax import lax
from jax.experimental import pallas as pl
from jax.experimental.pallas import tpu as pltpu

F32 = jnp.float32
BF16 = jnp.bfloat16

SB_HEADS = 8
SB_HEAD_DIM = 64
SB_WIDTH = SB_HEADS * SB_HEAD_DIM
SB_TILE = 128
GDN_HEADS = 4
GDN_DK = 128
GDN_DV = 128
GDN_WIDTH = GDN_HEADS * GDN_DV
GDN_CONV_DIM = GDN_HEADS * (2 * GDN_DK + GDN_DV)
CONV_W = 4
GDN_CHUNK = 64
AB_IN = 4 * SB_WIDTH + GDN_CONV_DIM + GDN_WIDTH + 2 * GDN_HEADS
RET_HEADS = 4
RET_CHUNK = 128
ROPE_BASE = 10000.0
XA_HEADS = 4
NORM_EPS = 1e-6
LANES = 128
SUBLANES = 8
VMEM_LIMIT = 56 * 1024 * 1024


def _cparams(*sem):
    return pltpu.CompilerParams(dimension_semantics=sem, vmem_limit_bytes=VMEM_LIMIT)


def _rms(x, g):
    ms = jnp.mean(x * x, axis=-1, keepdims=True)
    return x * lax.rsqrt(ms + NORM_EPS) * g


def _silu(x):
    return x * jax.nn.sigmoid(x)


def _softplus(x):
    return jnp.maximum(x, 0.0) + jnp.log1p(jnp.exp(-jnp.abs(x)))


def _dot(a, b):
    return jnp.dot(a.astype(BF16), b.astype(BF16), preferred_element_type=F32)


def _dot_nt(a, b):
    return lax.dot_general(a.astype(BF16), b.astype(BF16), (((1,), (1,)), ((), ())),
                           preferred_element_type=F32)


def _dot_tn(a, b):
    return lax.dot_general(a.astype(BF16), b.astype(BF16), (((0,), (0,)), ((), ())),
                           preferred_element_type=F32)


def _split_bf16(x):
    hi = x.astype(BF16)
    lo = (x - hi.astype(F32)).astype(BF16)
    return hi, lo


def _proj_kernel(x_ref, g_ref, w_ref, o_ref, *rest, norm, emit_bf16):
    h_ref = rest[-1]

    @pl.when(pl.program_id(1) == 0)
    def _():
        x = x_ref[...]
        if norm:
            x = _rms(x, g_ref[...])
        h_ref[...] = x.astype(BF16)

    res = jnp.dot(h_ref[...], w_ref[...], preferred_element_type=F32)
    o_ref[...] = res
    if emit_bf16:
        rest[0][...] = res.astype(BF16)


def _pick_tile(n, cap, unit):
    best = unit
    for t in range(unit, min(n, cap) + 1, unit):
        if n % t == 0:
            best = t
    return best


PROJ_ROWS = 1024


def _proj(x2d, g, w_bf16, *, norm, emit_bf16=False):
    m, d = x2d.shape
    n = w_bf16.shape[1]
    tm = m if m <= PROJ_ROWS else _pick_tile(m, PROJ_ROWS, SUBLANES)
    tn = _pick_tile(n, 2048, LANES)
    out_spec = pl.BlockSpec((tm, tn), lambda i, j: (i, j))
    out_shape = jax.ShapeDtypeStruct((m, n), F32)
    if emit_bf16:
        out_shape = (out_shape, jax.ShapeDtypeStruct((m, n), BF16))
        out_spec = (out_spec, pl.BlockSpec((tm, tn), lambda i, j: (i, j)))
    return pl.pallas_call(
        functools.partial(_proj_kernel, norm=norm, emit_bf16=emit_bf16),
        out_shape=out_shape,
        grid=(m // tm, n // tn),
        in_specs=[pl.BlockSpec((tm, d), lambda i, j: (i, 0)),
                  pl.BlockSpec((1, d), lambda i, j: (0, 0)),
                  pl.BlockSpec((d, tn), lambda i, j: (0, j))],
        out_specs=out_spec,
        scratch_shapes=[pltpu.VMEM((tm, d), BF16)],
        compiler_params=_cparams("parallel", "arbitrary"),
        name="proj",
    )(x2d, g.reshape(1, d), w_bf16)


def _suffix_matrix(n):
    j = np.arange(n)[:, None]
    s = np.arange(n)[None, :]
    return jnp.asarray((j >= s).astype(np.float32), dtype=BF16)


def _log_keep(z):
    return -(jnp.maximum(z, 0.0) + jnp.log(1.0 + jnp.exp(-jnp.abs(z))))


LOG2E = math.log2(math.e)


def _sb_prompt_kernel(bias_ref, q_ref, k_ref, v_ref, z_ref, tri_ref, o_ref, c_ref, acc_ref, qm_ref):
    i = pl.program_id(1)
    tq = q_ref.shape[1]
    tk = tq
    pairs = SB_HEADS // 2
    lane = lax.broadcasted_iota(jnp.int32, (tq, LANES), 1)
    first = lane < SB_HEAD_DIM
    row = lax.broadcasted_iota(jnp.int32, (2 * tq, tk), 0)
    col = lax.broadcasted_iota(jnp.int32, (2 * tq, tk), 1)
    causal = col < jnp.where(row >= tq, row - tq, row)
    tri = tri_ref[...]
    c_ref[...] = jnp.zeros_like(c_ref)
    acc_ref[...] = jnp.zeros_like(acc_ref)
    for p in range(pairs):
        qp = q_ref[0, :, p * LANES:(p + 1) * LANES] * (LOG2E * SB_HEAD_DIM ** -0.5)
        qm_ref[p, :tq] = jnp.where(first, qp, 0.0).astype(BF16)
        qm_ref[p, tq:] = jnp.where(first, 0.0, qp).astype(BF16)

    def tile(j, masked):
        start = pl.multiple_of(j * tk, tk)
        z2s, sps = [], []
        for p in range(pairs):
            kb = k_ref[0, pl.ds(start, tk), p * LANES:(p + 1) * LANES]
            s = lax.dot_general(qm_ref[p], kb, (((1,), (1,)), ((), ())), preferred_element_type=F32)
            z2 = jnp.concatenate([s[:tq] + bias_ref[2 * p] * LOG2E, s[tq:] + bias_ref[2 * p + 1] * LOG2E], axis=0)
            neg_abs = lax.bitcast_convert_type(
                lax.bitcast_convert_type(z2, jnp.uint32) | jnp.uint32(0x80000000), F32)
            sp = jnp.maximum(z2, 0.0) + jnp.log2(1.0 + jnp.exp2(neg_abs))
            if masked:
                sp = jnp.where(causal, sp, 0.0)
            z2s.append(z2)
            sps.append(sp)
        incl_all = jnp.dot(jnp.concatenate(sps, axis=0).astype(BF16), tri, preferred_element_type=F32)
        for p in range(pairs):
            vb = v_ref[0, pl.ds(start, tk), p * LANES:(p + 1) * LANES]
            incl = incl_all[p * 2 * tq:(p + 1) * 2 * tq]
            c = c_ref[p]
            a = jnp.exp2(z2s[p] - incl - jnp.concatenate([c] * (tk // LANES), axis=1))
            if masked:
                a = jnp.where(causal, a, 0.0)
            acc_ref[p] += jnp.dot(a.astype(BF16), vb, preferred_element_type=F32)
            c_ref[p] = c + jnp.broadcast_to(incl[:, 0:1], (2 * tq, LANES))

    tile(i, True)

    def body(jj, carry):
        tile(i - 1 - jj, False)
        return carry

    lax.fori_loop(0, i, body, 0)
    for p in range(pairs):
        cols = slice(p * LANES, (p + 1) * LANES)
        o = jnp.where(first, acc_ref[p, :tq], acc_ref[p, tq:])
        o_ref[0, :, cols] = o * _silu(z_ref[0, :, cols])


SB_PROMPT_TILE = 256


def _sb_prompt(p3, pb3, bias):
    b, t, _ = p3.shape
    tq = min(SB_PROMPT_TILE, t)
    assert t % tq == 0 and tq % LANES == 0
    return pl.pallas_call(
        _sb_prompt_kernel,
        out_shape=jax.ShapeDtypeStruct((b, t, SB_WIDTH), F32),
        grid=(b, t // tq),
        in_specs=[pl.BlockSpec(memory_space=pltpu.SMEM),
                  pl.BlockSpec((1, tq, SB_WIDTH), lambda bi, i: (bi, i, 0)),
                  pl.BlockSpec((1, t, SB_WIDTH), lambda bi, i: (bi, 0, 1)),
                  pl.BlockSpec((1, t, SB_WIDTH), lambda bi, i: (bi, 0, 2)),
                  pl.BlockSpec((1, tq, SB_WIDTH), lambda bi, i: (bi, i, 3)),
                  pl.BlockSpec((tq, tq), lambda bi, i: (0, 0))],
        out_specs=pl.BlockSpec((1, tq, SB_WIDTH), lambda bi, i: (bi, i, 0)),
        scratch_shapes=[pltpu.VMEM((SB_HEADS // 2, 2 * tq, LANES), F32),
                        pltpu.VMEM((SB_HEADS // 2, 2 * tq, LANES), F32),
                        pltpu.VMEM((SB_HEADS // 2, 2 * tq, LANES), BF16)],
        compiler_params=_cparams("parallel", "arbitrary"),
        name="sb_prompt",
    )(bias, p3, pb3, pb3, p3, _suffix_matrix(tq))


SB_PAGES_PER_STEP = 16


def _tree_sum(xs):
    while len(xs) > 1:
        xs = [xs[k] + xs[k + 1] for k in range(0, len(xs) - 1, 2)] + ([xs[-1]] if len(xs) % 2 else [])
    return xs[0]


def _fold_sublanes(tiles):
    assert len(tiles) == SUBLANES
    order = [0, 4, 2, 6, 1, 5, 3, 7]
    cur = [tiles[k] for k in order]
    sub = lax.broadcasted_iota(jnp.int32, cur[0].shape, 0)
    s = SUBLANES // 2
    while s >= 1:
        low = (sub // s) % 2 == 0
        nxt = []
        for k in range(0, len(cur), 2):
            a, b = cur[k], cur[k + 1]
            fa = a + pltpu.roll(a, SUBLANES - s, axis=0)
            fb = b + pltpu.roll(b, s, axis=0)
            nxt.append(jnp.where(low, fa, fb))
        cur = nxt
        s //= 2
    return cur[0]


def _sb_decode_kernel(pt_ref, qb_ref, z_ref, bias_ref, tri_ref, *rest):
    del pt_ref
    g = SB_PAGES_PER_STEP
    k_refs = rest[:g]
    v_refs = rest[g:2 * g]
    o_ref = rest[2 * g]
    c_ref, acc_ref, a_ref, qs_ref = rest[2 * g + 1:]
    step = pl.program_id(1)
    page = k_refs[0].shape[3]

    @pl.when(step == 0)
    def _():
        c_ref[...] = jnp.zeros_like(c_ref)
        acc_ref[...] = jnp.zeros_like(acc_ref)
        qs_ref[...] = qb_ref[0] * (SB_HEAD_DIM ** -0.5)

    parts = [[None] * SB_HEADS for _ in range(g)]
    for h in range(SB_HEADS):
        rows = slice(h * SB_HEAD_DIM, (h + 1) * SB_HEAD_DIM)
        qh = qs_ref[rows, :]
        for i in range(g):
            prod = k_refs[i][0, 0, rows, :] * qh
            parts[i][h] = _tree_sum([prod[r:r + SUBLANES] for r in range(0, SB_HEAD_DIM, SUBLANES)])
    bias = bias_ref[...]
    zs = [_fold_sublanes(parts[i]) + bias for i in range(g)]
    lk = jnp.concatenate([_log_keep(z) for z in zs], axis=0)
    hi, lo = _split_bf16(lk)
    tri = tri_ref[...]
    r = jnp.dot(hi, tri, preferred_element_type=F32) + jnp.dot(lo, tri, preferred_element_type=F32)
    c = c_ref[...]
    for i in range(g):
        rows = slice(i * SB_HEADS, (i + 1) * SB_HEADS)
        a_ref[i] = jnp.exp(zs[i] + r[rows, :page] + c)
        c = c + r[rows, page:]
    c_ref[...] = c
    for h in range(SB_HEADS):
        rows = slice(h * SB_HEAD_DIM, (h + 1) * SB_HEAD_DIM)
        acc = acc_ref[rows, :]
        for i in range(g):
            acc = acc + v_refs[i][0, 0, rows, :] * jnp.broadcast_to(a_ref[i, h:h + 1, :], (SB_HEAD_DIM, page))
        acc_ref[rows, :] = acc

    @pl.when(step == pl.num_programs(1) - 1)
    def _():
        hi, lo = _split_bf16(acc_ref[...])
        ones = jnp.ones((SUBLANES, page), BF16)
        nt = (((1,), (1,)), ((), ()))
        o = (lax.dot_general(ones, hi, nt, preferred_element_type=F32)
             + lax.dot_general(ones, lo, nt, preferred_element_type=F32))
        rid = lax.broadcasted_iota(jnp.int32, o.shape, 0)
        o_ref[0] = jnp.where(rid == 0, o, 0.0) * _silu(z_ref[0])


def _sb_decode(p3, bias, cache_kt, cache_vt, layer, page_table):
    b, rows, _ = p3.shape
    n_pages = page_table.shape[1]
    page = cache_kt.shape[3]
    g = SB_PAGES_PER_STEP
    assert n_pages % g == 0 and page == LANES and rows == SUBLANES
    steps = n_pages // g

    def page_spec(i):
        return pl.BlockSpec((1, 1, SB_WIDTH, page),
                            lambda bi, s, pt: (layer, pt[bi, n_pages - 1 - (s * g + i)], 0, 0))

    bias_b = jnp.broadcast_to(bias.astype(F32)[:, None], (SB_HEADS, page))
    tri = jnp.concatenate([_suffix_matrix(page), jnp.ones((page, page), BF16)], axis=1)
    q_b =jnp.broadcast_to(p3[:, 0, :SB_WIDTH, None], (b, SB_WIDTH, page))
    grid_spec = pltpu.PrefetchScalarGridSpec(
        num_scalar_prefetch=1,
        grid=(b, steps),
        in_specs=[pl.BlockSpec((1, SB_WIDTH, page), lambda bi, s, pt: (bi, 0, 0)),
                  pl.BlockSpec((1, rows, SB_WIDTH), lambda bi, s, pt: (bi, 0, 3)),
                  pl.BlockSpec((SB_HEADS, page), lambda bi, s, pt: (0, 0)),
                  pl.BlockSpec((page, 2 * page), lambda bi, s, pt: (0, 0))]
        + [page_spec(i) for i in range(g)] + [page_spec(i) for i in range(g)],
        out_specs=pl.BlockSpec((1, rows, SB_WIDTH), lambda bi, s, pt: (bi, 0, 0)),
        scratch_shapes=[pltpu.VMEM((SB_HEADS, page), F32), pltpu.VMEM((SB_WIDTH, page), F32),
                        pltpu.VMEM((g, SB_HEADS, page), F32), pltpu.VMEM((SB_WIDTH, page), F32)],
    )
    return pl.pallas_call(
        _sb_decode_kernel,
        out_shape=jax.ShapeDtypeStruct((b, rows, SB_WIDTH), F32),
        grid_spec=grid_spec,
        compiler_params=_cparams("parallel", "arbitrary"),
        name="sb_decode",
    )(page_table, q_b, p3, bias_b, tri, *([cache_kt] * g), *([cache_vt] * g))


def _gdn_kernel(hp_ref, q_ref, k_ref, v_ref, gz_ref, gate_ref, cw_ref, ng_ref, buf_ref, s0_ref,
                o_ref, sout_ref, s_ref, tail_ref, *, t_true):
    n = pl.program_id(1)
    bb, c = q_ref.shape[0], q_ref.shape[1]
    nh = GDN_HEADS
    hc = nh * c

    @pl.when(n == 0)
    def _():
        tail_ref[...] = buf_ref[...]
        for bi in range(bb):
            for h in range(nh):
                s_ref[bi, :, h * GDN_DV:(h + 1) * GDN_DV] = s0_ref[bi, h]

    rowid = lax.broadcasted_iota(jnp.int32, (c, 1), 0) + n * c
    valid = (rowid < t_true).astype(F32)
    ri = lax.broadcasted_iota(jnp.int32, (hc, hc), 0)
    ci = lax.broadcasted_iota(jnp.int32, (hc, hc), 1)
    start = ri // c * c
    in_head = jnp.logical_and(ci >= start, ci < start + c)
    consts = dict(
        valid=valid, cw=cw_ref[...],
        lower_incl=jnp.logical_and(in_head, ci <= ri),
        lower_strict=jnp.logical_and(in_head, ci < ri),
        upper_incl=jnp.logical_and(in_head, ci >= ri),
        eye=(ri == ci).astype(F32),
        ones8=jnp.ones((SUBLANES, hc), BF16),
        own_block=(lax.broadcasted_iota(jnp.int32, (hc, nh * GDN_DV), 0) // c
                   == lax.broadcasted_iota(jnp.int32, (hc, nh * GDN_DV), 1) // GDN_DV))
    chains = [_gdn_chunk(bi, hp_ref, q_ref, k_ref, v_ref, gz_ref, gate_ref, ng_ref, o_ref, s_ref, tail_ref, consts)
              for bi in range(bb)]
    for _ in zip(*chains):
        pass
    for chain in chains:
        for _ in chain:
            pass

    @pl.when(n == pl.num_programs(1) - 1)
    def _():
        for bi in range(bb):
            for h in range(nh):
                sout_ref[bi, h] = s_ref[bi, :, h * GDN_DV:(h + 1) * GDN_DV]


def _gdn_chunk(bi, hp_ref, q_ref, k_ref, v_ref, gz_ref, gate_ref, ng_ref, o_ref, s_ref, tail_ref, k_):
    c = q_ref.shape[1]
    nh = GDN_HEADS
    hc = nh * c
    hw = nh * GDN_DK
    valid, cw = k_["valid"], k_["cw"]
    tail = tail_ref[bi]

    def conv(x_ref, sec):
        x = x_ref[bi]
        xx = jnp.concatenate([tail[:, sec * hw:(sec + 1) * hw], x], axis=0)
        w = cw[:, sec * hw:(sec + 1) * hw]
        y = w[CONV_W - 1:CONV_W] * x
        for i in range(CONV_W - 1):
            off = SUBLANES - (CONV_W - 1) + i
            y = y + w[i:i + 1] * xx[off:off + c]
        tail_ref[bi, :, sec * hw:(sec + 1) * hw] = x[c - SUBLANES:, :]
        return _silu(y) * valid

    def stack(x):
        return jnp.concatenate([x[:, h * GDN_DK:(h + 1) * GDN_DK] for h in range(nh)], axis=0)

    def own(x):
        return jnp.concatenate([x[h * c:(h + 1) * c, h * GDN_DV:(h + 1) * GDN_DV] for h in range(nh)], axis=0)

    def l2n(x):
        return x * lax.rsqrt(jnp.sum(x * x, axis=-1, keepdims=True) + NORM_EPS)

    q_s = l2n(stack(conv(q_ref, 0))) * (GDN_DK ** -0.5)
    k_s = l2n(stack(conv(k_ref, 1)))
    v_s = stack(conv(v_ref, 2))
    gates = gate_ref[bi]
    g_blk = -jnp.exp(hp_ref[0:1, :]) * _softplus(gates + hp_ref[1:2, :]) * valid
    b_blk = jax.nn.sigmoid(gates) * valid
    wide = max(hc, LANES)
    g_w = jnp.concatenate([jnp.broadcast_to(g_blk[:, h:h + 1], (c, wide)) for h in range(nh)], axis=0)
    beta = jnp.concatenate([jnp.broadcast_to(b_blk[:, nh + h:nh + h + 1], (c, LANES)) for h in range(nh)], axis=0)
    g_hi, g_lo = _split_bf16(g_w)
    tri = k_["lower_incl"].astype(BF16)
    gc_w = jnp.dot(tri, g_hi, preferred_element_type=F32) + jnp.dot(tri, g_lo, preferred_element_type=F32)
    u_hi, u_lo = _split_bf16(jnp.where(k_["upper_incl"], g_w[:, :hc], 0.0))
    grow = (jnp.dot(k_["ones8"], u_hi, preferred_element_type=F32)
            + jnp.dot(k_["ones8"], u_lo, preferred_element_type=F32))[0:1, :]
    yield
    decay = jnp.where(k_["lower_incl"], jnp.exp(gc_w[:, :hc] - grow), 0.0)
    gc = gc_w[:, :LANES]
    e_gc = jnp.exp(gc)
    g_last = jnp.concatenate([jnp.broadcast_to(gc[(h + 1) * c - 1:(h + 1) * c], (c, LANES)) for h in range(nh)],
                             axis=0)
    kb = k_s * beta
    aq = _dot_nt(jnp.concatenate([kb, q_s], axis=0), k_s)
    yield
    m = jnp.where(k_["lower_strict"], aq[:hc] * decay, 0.0)
    attn = aq[hc:] * decay
    pw = -m
    tinv = k_["eye"] + pw
    levels = c.bit_length() - 1
    if levels >= 2:
        pw = _dot(pw, pw)
        yield
    for r in range(1, levels):
        if r < levels - 1:
            both = _dot(jnp.concatenate([tinv, pw], axis=0), pw)
            yield
            tinv = tinv + both[:hc]
            pw = both[hc:]
        else:
            tinv = tinv + _dot(tinv, pw)
            yield
    uw = _dot(tinv, jnp.concatenate([v_s * beta, kb * e_gc], axis=1))
    yield
    s_old = s_ref[bi]
    ws = _dot(jnp.concatenate([uw[:, GDN_DV:], q_s * e_gc], axis=0), s_old)
    yield
    v_new = uw[:, :GDN_DV] - own(ws[:hc])
    o = own(ws[hc:]) + _dot(attn, v_new)
    yield
    kd = k_s * jnp.exp(g_last - gc)
    v_bd = jnp.where(k_["own_block"], jnp.concatenate([v_new] * nh, axis=1), 0.0)
    e_last = jnp.concatenate([jnp.exp(gc[(h + 1) * c - 1:(h + 1) * c]) for h in range(nh)], axis=1)
    s_ref[bi] = s_old * e_last + _dot_tn(kd, v_bd)
    o = _rms(o, ng_ref[...]) * _silu(stack(gz_ref[bi]))
    for h in range(nh):
        o_ref[bi, :, h * GDN_DV:(h + 1) * GDN_DV] = o[h * c:(h + 1) * c]


GDN_SEQS_PER_STEP = 4


def _gdn(p3, conv_w, a_log, dt_bias, norm_g, buf8, s0, t_true):
    b, tp, _ = p3.shape
    c = min(GDN_CHUNK, tp)
    bb = math.gcd(b, GDN_SEQS_PER_STEP)
    hw = GDN_HEADS * GDN_DK
    base = 4 * SB_WIDTH // hw
    kern = functools.partial(_gdn_kernel, t_true=t_true)
    head_params = jnp.pad(jnp.stack([a_log, dt_bias]).astype(F32), ((0, 0), (0, LANES - GDN_HEADS)))
    return pl.pallas_call(
        kern,
        out_shape=(jax.ShapeDtypeStruct((b, tp, GDN_WIDTH), F32),
                   jax.ShapeDtypeStruct((b, GDN_HEADS, GDN_DK, GDN_DV), F32)),
        grid=(b // bb, tp // c),
        in_specs=[pl.BlockSpec((2, LANES), lambda bi, n: (0, 0)),
                  pl.BlockSpec((bb, c, hw), lambda bi, n: (bi, n, base)),
                  pl.BlockSpec((bb, c, hw), lambda bi, n: (bi, n, base + 1)),
                  pl.BlockSpec((bb, c, hw), lambda bi, n: (bi, n, base + 2)),
                  pl.BlockSpec((bb, c, GDN_WIDTH), lambda bi, n: (bi, n, base + 3)),
                  pl.BlockSpec((bb, c, LANES), lambda bi, n: (bi, n, (4 * SB_WIDTH + GDN_CONV_DIM + GDN_WIDTH) // LANES)),
                  pl.BlockSpec((CONV_W, GDN_CONV_DIM), lambda bi, n: (0, 0)),
                  pl.BlockSpec((1, GDN_DV), lambda bi, n: (0, 0)),
                  pl.BlockSpec((bb, SUBLANES, GDN_CONV_DIM), lambda bi, n: (bi, 0, 0)),
                  pl.BlockSpec((bb, GDN_HEADS, GDN_DK, GDN_DV), lambda bi, n: (bi, 0, 0, 0))],
        out_specs=(pl.BlockSpec((bb, c, GDN_WIDTH), lambda bi, n: (bi, n, 0)),
                   pl.BlockSpec((bb, GDN_HEADS, GDN_DK, GDN_DV), lambda bi, n: (bi, 0, 0, 0))),
        scratch_shapes=[pltpu.VMEM((bb, GDN_DK, GDN_HEADS * GDN_DV), F32),
                        pltpu.VMEM((bb, SUBLANES, GDN_CONV_DIM), F32)],
        compiler_params=_cparams("parallel", "arbitrary"),
        name="gdn",
    )(head_params, p3, p3, p3, p3, p3, conv_w, norm_g.reshape(1, GDN_DV), buf8, s0)


def _ret_kernel(q_ref, k_ref, v_ref, z_ref, ng_ref, s0_ref, o_ref, sout_ref, s_ref, *, t_true, c_true, past_len):
    n = pl.program_id(1)
    c = q_ref.shape[1]
    dk = q_ref.shape[2] // RET_HEADS
    dv = v_ref.shape[2] // RET_HEADS
    half = dk // 2

    @pl.when(n == 0)
    def _():
        s_ref[...] = s0_ref[0]

    rid = lax.broadcasted_iota(jnp.int32, (c, 1), 0)
    valid = ((rid + n * c) < t_true).astype(F32)
    pos = (rid + (n * c + past_len)).astype(F32)
    fidx = lax.broadcasted_iota(jnp.int32, (1, half), 1).astype(F32)
    inv = jnp.exp(fidx * (-math.log(ROPE_BASE) / half))
    ang = pos * inv
    cos, sin = jnp.cos(ang), jnp.sin(ang)
    ii = lax.broadcasted_iota(jnp.int32, (c, c), 0)
    jj = lax.broadcasted_iota(jnp.int32, (c, c), 1)
    diff = (ii - jj).astype(F32)
    idx = rid.astype(F32)

    def rot(x):
        x1, x2 = x[:, :half], x[:, half:]
        return jnp.concatenate([x1 * cos - x2 * sin, x2 * cos + x1 * sin], axis=-1)

    def head(h):
        lg = math.log1p(-(2.0 ** (-5.0 - h)))
        qh = rot(q_ref[0, :, h * dk:(h + 1) * dk]) * valid
        kh = rot(k_ref[0, :, h * dk:(h + 1) * dk]) * (valid * (dk ** -0.5))
        vh = v_ref[0, :, h * dv:(h + 1) * dv] * valid
        dmask = jnp.where(ii >= jj, jnp.exp(lg * jnp.maximum(diff, 0.0)), 0.0)
        xi = jnp.exp(lg * (idx + 1.0))
        zeta = jnp.exp(lg * (c_true - 1.0 - idx))
        s_old = s_ref[h]
        inner = _dot_nt(qh, kh)
        yield
        o_state = _dot(qh * xi, s_old)
        yield
        o = _dot(inner * dmask, vh) + o_state
        yield
        s_ref[h] = math.exp(lg * c_true) * s_old + _dot_tn(kh * zeta, vh)
        yield
        o = _rms(o, ng_ref[...]) * _silu(z_ref[0, :, h * dv:(h + 1) * dv])
        o_ref[0, :, h * dv:(h + 1) * dv] = o

    chains = [head(h) for h in range(RET_HEADS)]
    for _ in zip(*chains):
        pass
    for chain in chains:
        for _ in chain:
            pass

    @pl.when(n == pl.num_programs(1) - 1)
    def _():
        sout_ref[0] = s_ref[...]


def _retention(p3, norm_g, s0, t_true, past_len):
    b, tp, width = p3.shape
    d = width // 6
    dk, dv = d // RET_HEADS, 2 * d // RET_HEADS
    c = min(RET_CHUNK, tp)
    kern = functools.partial(_ret_kernel, t_true=t_true, c_true=min(RET_CHUNK, t_true), past_len=past_len)
    return pl.pallas_call(
        kern,
        out_shape=(jax.ShapeDtypeStruct((b, tp, 2 * d), F32),
                   jax.ShapeDtypeStruct((b, RET_HEADS, dk, dv), F32)),
        grid=(b, tp // c),
        in_specs=[pl.BlockSpec((1, c, d), lambda bi, n: (bi, n, 0)),
                  pl.BlockSpec((1, c, d), lambda bi, n: (bi, n, 1)),
                  pl.BlockSpec((1, c, 2 * d), lambda bi, n: (bi, n, 1)),
                  pl.BlockSpec((1, c, 2 * d), lambda bi, n: (bi, n, 2)),
                  pl.BlockSpec((1, dv), lambda bi, n: (0, 0)),
                  pl.BlockSpec((1, RET_HEADS, dk, dv), lambda bi, n: (bi, 0, 0, 0))],
        out_specs=(pl.BlockSpec((1, c, 2 * d), lambda bi, n: (bi, n, 0)),
                   pl.BlockSpec((1, RET_HEADS, dk, dv), lambda bi, n: (bi, 0, 0, 0))),
        scratch_shapes=[pltpu.VMEM((RET_HEADS, dk, dv), F32)],
        compiler_params=_cparams("parallel", "arbitrary"),
        name="retention",
    )(p3, p3, p3, p3, norm_g.reshape(1, dv), s0)


def _post_kernel(*refs, n_mix, final):
    x_ref = refs[0]
    mix_refs = refs[1:1 + n_mix]
    w_refs = refs[1 + n_mix:1 + 2 * n_mix]
    gxa_ref, wq_ref, mk_ref, mv_ref, wo_ref, gf_ref, o_ref = refs[1 + 2 * n_mix:]
    bs, tm, d = x_ref.shape
    rows = bs * tm
    y = x_ref[...].reshape(rows, d)
    for m_ref, w_ref in zip(mix_refs, w_refs):
        y = y + jnp.dot(m_ref[...].reshape(rows, m_ref.shape[2]).astype(BF16), w_ref[...],
                        preferred_element_type=F32)
    h = _rms(y, gxa_ref[...])
    q = jnp.dot(h.astype(BF16), wq_ref[...], preferred_element_type=F32)
    hd = d // XA_HEADS
    out = [[None] * XA_HEADS for _ in range(bs)]

    def attend(s, i):
        qs = q[s * tm:(s + 1) * tm, i * hd:(i + 1) * hd]
        sc = _dot_nt(qs, mk_ref[s, i]) * (hd ** -0.5)
        yield
        sc = sc - jnp.max(sc, axis=-1, keepdims=True)
        p = jnp.exp(sc)
        p = p / jnp.sum(p, axis=-1, keepdims=True)
        out[s][i] = _dot(p, mv_ref[s, i])
        yield

    chains = [attend(s, i) for s in range(bs) for i in range(XA_HEADS)]
    for _ in zip(*chains):
        pass
    o = jnp.concatenate([jnp.concatenate(out[s], axis=-1) for s in range(bs)], axis=0)
    y = y + jnp.dot(o.astype(BF16), wo_ref[...], preferred_element_type=F32)
    if final:
        y = _rms(y, gf_ref[...])
    o_ref[...] = y.reshape(bs, tm, d)


POST_ROWS = 256
POST_SEQS_SHORT = 4


def _post(x, mixes, g_xa, wq, mem_k, mem_v, wo, g_final, final):
    b, tp, d = x.shape
    tm = min(tp, POST_ROWS)
    bs = math.gcd(b, POST_SEQS_SHORT) if tp == SUBLANES else 1
    mem_block = (bs,) + mem_k.shape[1:]
    row = lambda bi, i: (bi, i, 0)
    const = lambda bi, i: (0, 0)
    in_specs = [pl.BlockSpec((bs, tm, d), row)]
    in_specs += [pl.BlockSpec((bs, tm, a.shape[2]), row) for a, _ in mixes]
    in_specs += [pl.BlockSpec(w.shape, const) for _, w in mixes]
    in_specs += [pl.BlockSpec((1, d), const), pl.BlockSpec((d, d), const),
                 pl.BlockSpec(mem_block, lambda bi, i: (bi, 0, 0, 0)),
                 pl.BlockSpec(mem_block, lambda bi, i: (bi, 0, 0, 0)),
                 pl.BlockSpec((d, d), const), pl.BlockSpec((1, d), const)]
    return pl.pallas_call(
        functools.partial(_post_kernel, n_mix=len(mixes), final=final),
        out_shape=jax.ShapeDtypeStruct((b, tp, d), F32),
        grid=(b // bs, tp // tm),
        in_specs=in_specs,
        out_specs=pl.BlockSpec((bs, tm, d), row),
        compiler_params=_cparams("parallel", "arbitrary"),
        name="post",
    )(x, *[a for a, _ in mixes], *[w for _, w in mixes], g_xa.reshape(1, d), wq, mem_k, mem_v, wo,
      g_final.reshape(1, d))


def _prep_weights(w_in_ab, w_out_ab, w_in_c, w_out_c, xa_wq, xa_wo):
    n_ab_pad = -(-AB_IN // LANES) * LANES
    return dict(
        w_in_ab=jnp.pad(w_in_ab, ((0, 0), (0, 0), (0, n_ab_pad - AB_IN))).astype(BF16),
        w_out_ab=w_out_ab.astype(BF16), w_in_c=w_in_c.astype(BF16), w_out_c=w_out_c.astype(BF16),
        xa_wq=xa_wq.astype(BF16), xa_wo=xa_wo.astype(BF16))


def _trunk(x, past_len, sb_fn, conv_bufs, gdn_states, ret_states, mem_k, mem_v, wts,
           norm_mix_g, norm_xa_g, sb_logit_bias, conv_w_gdn, gdn_a_log, gdn_dt_bias, gdn_norm_g,
           ret_norm_g, norm_final_g):
    b, t, d = x.shape
    tp = -(-t // SUBLANES) * SUBLANES
    if tp != t:
        x = jnp.pad(x, ((0, 0), (0, tp - t), (0, 0)))
    depth = norm_mix_g.shape[0]
    ks, vs, convs, gdns, rets = [], [], [], [], []
    for layer in range(depth):
        i = layer // 2
        last = layer == depth - 1
        if layer % 2 == 0:
            p, pb = _proj(x.reshape(b * tp, d), norm_mix_g[layer], wts["w_in_ab"][i], norm=True, emit_bf16=True)
            p = p.reshape(b, tp, -1)
            o_sb = sb_fn(i, p, pb.reshape(b, tp, -1))
            buf8 = jnp.pad(conv_bufs[i], ((0, 0), (SUBLANES - (CONV_W - 1), 0), (0, 0)))
            o_g, s_gdn = _gdn(p, conv_w_gdn[i], gdn_a_log[i], gdn_dt_bias[i], gdn_norm_g[i], buf8, gdn_states[i], t)
            ks.append(p[:, :t, SB_WIDTH:2 * SB_WIDTH].reshape(b, t, SB_HEADS, SB_HEAD_DIM))
            vs.append(p[:, :t, 2 * SB_WIDTH:3 * SB_WIDTH].reshape(b, t, SB_HEADS, SB_HEAD_DIM))
            qkv = p[:, :t, 4 * SB_WIDTH:4 * SB_WIDTH + GDN_CONV_DIM]
            keep = CONV_W - 1
            convs.append(jnp.concatenate([conv_bufs[i], qkv[:, max(t - keep, 0):]], axis=1)[:, -keep:])
            gdns.append(s_gdn)
            mixes = [(o_sb, wts["w_out_ab"][i, :SB_WIDTH]), (o_g, wts["w_out_ab"][i, SB_WIDTH:])]
        else:
            p = _proj(x.reshape(b * tp, d), norm_mix_g[layer], wts["w_in_c"][i], norm=True).reshape(b, tp, -1)
            o_r, s_ret = _retention(p, ret_norm_g[i], ret_states[i], t, past_len)
            rets.append(s_ret)
            mixes = [(o_r, wts["w_out_c"][i])]
        x = _post(x, mixes, norm_xa_g[layer], wts["xa_wq"][layer], mem_k[layer], mem_v[layer],
                  wts["xa_wo"][layer], norm_final_g, last)
    return (x[:, :t], jnp.stack(ks), jnp.stack(vs), jnp.stack(gdns), jnp.stack(convs), jnp.stack(rets))


def kernel(x_prompt, x_sample, mem_prompt, cache_sb_k, cache_sb_v, state_gdn, state_gdn_conv, state_ret,
           cache_mem_k, cache_mem_v, page_table, norm_mix_g, norm_xa_g, w_in_ab, sb_logit_bias, conv_w_gdn,
           gdn_a_log, gdn_dt_bias, gdn_norm_g, w_out_ab, w_in_c, ret_norm_g, w_out_c, xa_wq, xa_wk, xa_wv,
           xa_wo, norm_final_g):
    wts = _prep_weights(w_in_ab, w_out_ab, w_in_c, w_out_c, xa_wq, xa_wo)
    shared = (norm_mix_g, norm_xa_g, sb_logit_bias, conv_w_gdn, gdn_a_log, gdn_dt_bias, gdn_norm_g,
              ret_norm_g, norm_final_g)
    depth = norm_mix_g.shape[0]
    n_ab, n_c = state_gdn.shape[0], state_ret.shape[0]

    bp, mlen, d = mem_prompt.shape
    hd = d // XA_HEADS
    w_kv = jnp.concatenate([xa_wk, xa_wv], axis=0).astype(BF16)
    w_kv = jnp.moveaxis(w_kv, 0, 1).reshape(d, 2 * depth * d)
    mem_kv = _proj(mem_prompt.reshape(bp * mlen, d), jnp.ones((d,), F32), w_kv, norm=False)
    mem_kv = mem_kv.reshape(bp, mlen, 2 * depth, XA_HEADS, hd)
    mem_hm = jnp.transpose(mem_kv, (2, 0, 3, 1, 4))
    mem_kv = jnp.moveaxis(mem_kv, 2, 0)
    mem_k_prompt, mem_v_prompt = mem_kv[:depth], mem_kv[depth:]
    conv0 = jnp.zeros((n_ab, bp) + state_gdn_conv.shape[2:], F32)
    gdn0 = jnp.zeros((n_ab, bp) + state_gdn.shape[2:], F32)
    ret0 = jnp.zeros((n_c, bp) + state_ret.shape[2:], F32)
    prompt = _trunk(x_prompt, 0, lambda i, p, pb: _sb_prompt(p, pb, sb_logit_bias[i]), conv0, gdn0, ret0,
                    mem_hm[:depth], mem_hm[depth:], wts, *shared)

    n_pages, page = page_table.shape[1], cache_sb_k.shape[2]
    assert x_sample.shape[1] == 1
    n_phys = cache_sb_k.shape[1]
    ckt = jnp.transpose(cache_sb_k, (0, 1, 3, 4, 2)).reshape(n_ab, n_phys, SB_WIDTH, page)
    cvt = jnp.transpose(cache_sb_v, (0, 1, 3, 4, 2)).reshape(n_ab, n_phys, SB_WIDTH, page)
    smem_k = jnp.transpose(cache_mem_k, (0, 1, 3, 2, 4))
    smem_v = jnp.transpose(cache_mem_v, (0, 1, 3, 2, 4))
    sample = _trunk(x_sample, n_pages * page,
                    lambda i, p, pb: _sb_decode(p, sb_logit_bias[i], ckt, cvt, i, page_table),
                    state_gdn_conv, state_gdn, state_ret, smem_k, smem_v, wts, *shared)

    y_p, k_p, v_p, gdn_p, conv_p, ret_p = prompt
    y_s, k_s, v_s, gdn_s, conv_s, ret_s = sample
    return (y_p, y_s, k_p, v_p, gdn_p, conv_p, ret_p, mem_k_prompt, mem_v_prompt,
            k_s, v_s, gdn_s, conv_s, ret_s)
```

```python
import functools
import math

import numpy as np
import jax
import jax.numpy as jnp
from jax import lax
from jax.experimental import pallas as pl
from jax.experimental.pallas import tpu as pltpu

F32 = jnp.float32
BF16 = jnp.bfloat16

SB_HEADS = 8
SB_HEAD_DIM = 64
SB_WIDTH = SB_HEADS * SB_HEAD_DIM
SB_TILE = 128
GDN_HEADS = 4
GDN_DK = 128
GDN_DV = 128
GDN_WIDTH = GDN_HEADS * GDN_DV
GDN_CONV_DIM = GDN_HEADS * (2 * GDN_DK + GDN_DV)
CONV_W = 4
GDN_CHUNK = 64
AB_IN = 4 * SB_WIDTH + GDN_CONV_DIM + GDN_WIDTH + 2 * GDN_HEADS
RET_HEADS = 4
RET_CHUNK = 128
ROPE_BASE = 10000.0
XA_HEADS = 4
NORM_EPS = 1e-6
LANES = 128
SUBLANES = 8
VMEM_LIMIT = 56 * 1024 * 1024


def _cparams(*sem):
    return pltpu.CompilerParams(dimension_semantics=sem, vmem_limit_bytes=VMEM_LIMIT)


def _rms(x, g):
    ms = jnp.mean(x * x, axis=-1, keepdims=True)
    return x * lax.rsqrt(ms + NORM_EPS) * g


def _silu(x):
    return x * jax.nn.sigmoid(x)


def _softplus(x):
    return jnp.maximum(x, 0.0) + jnp.log1p(jnp.exp(-jnp.abs(x)))


def _dot(a, b):
    return jnp.dot(a.astype(BF16), b.astype(BF16), preferred_element_type=F32)


def _dot_nt(a, b):
    return lax.dot_general(a.astype(BF16), b.astype(BF16), (((1,), (1,)), ((), ())),
                           preferred_element_type=F32)


def _dot_tn(a, b):
    return lax.dot_general(a.astype(BF16), b.astype(BF16), (((0,), (0,)), ((), ())),
                           preferred_element_type=F32)


def _split_bf16(x):
    hi = x.astype(BF16)
    lo = (x - hi.astype(F32)).astype(BF16)
    return hi, lo


def _proj_kernel(x_ref, g_ref, w_ref, o_ref, *rest, norm, emit_bf16):
    h_ref = rest[-1]

    @pl.when(pl.program_id(1) == 0)
    def _():
        x = x_ref[...]
        if norm:
            x = _rms(x, g_ref[...])
        h_ref[...] = x.astype(BF16)

    res = jnp.dot(h_ref[...], w_ref[...], preferred_element_type=F32)
    o_ref[...] = res
    if emit_bf16:
        rest[0][...] = res.astype(BF16)


def _pick_tile(n, cap, unit):
    best = unit
    for t in range(unit, min(n, cap) + 1, unit):
        if n % t == 0:
            best = t
    return best


PROJ_ROWS = 1024


def _proj(x2d, g, w_bf16, *, norm, emit_bf16=False):
    m, d = x2d.shape
    n = w_bf16.shape[1]
    tm = m if m <= PROJ_ROWS else _pick_tile(m, PROJ_ROWS, SUBLANES)
    tn = _pick_tile(n, 2048, LANES)
    out_spec = pl.BlockSpec((tm, tn), lambda i, j: (i, j))
    out_shape = jax.ShapeDtypeStruct((m, n), F32)
    if emit_bf16:
        out_shape = (out_shape, jax.ShapeDtypeStruct((m, n), BF16))
        out_spec = (out_spec, pl.BlockSpec((tm, tn), lambda i, j: (i, j)))
    return pl.pallas_call(
        functools.partial(_proj_kernel, norm=norm, emit_bf16=emit_bf16),
        out_shape=out_shape,
        grid=(m // tm, n // tn),
        in_specs=[pl.BlockSpec((tm, d), lambda i, j: (i, 0)),
                  pl.BlockSpec((1, d), lambda i, j: (0, 0)),
                  pl.BlockSpec((d, tn), lambda i, j: (0, j))],
        out_specs=out_spec,
        scratch_shapes=[pltpu.VMEM((tm, d), BF16)],
        compiler_params=_cparams("parallel", "arbitrary"),
        name="proj",
    )(x2d, g.reshape(1, d), w_bf16)


def _suffix_matrix(n):
    j = np.arange(n)[:, None]
    s = np.arange(n)[None, :]
    return jnp.asarray((j >= s).astype(np.float32), dtype=BF16)


def _log_keep(z):
    return -(jnp.maximum(z, 0.0) + jnp.log(1.0 + jnp.exp(-jnp.abs(z))))


LOG2E = math.log2(math.e)


def _sb_prompt_kernel(bias_ref, q_ref, k_ref, v_ref, z_ref, tri_ref, o_ref, c_ref, acc_ref, qm_ref):
    i = pl.program_id(1)
    tq = q_ref.shape[1]
    tk = tq
    pairs = SB_HEADS // 2
    lane = lax.broadcasted_iota(jnp.int32, (tq, LANES), 1)
    first = lane < SB_HEAD_DIM
    row = lax.broadcasted_iota(jnp.int32, (2 * tq, tk), 0)
    col = lax.broadcasted_iota(jnp.int32, (2 * tq, tk), 1)
    causal = col < jnp.where(row >= tq, row - tq, row)
    tri = tri_ref[...]
    c_ref[...] = jnp.zeros_like(c_ref)
    acc_ref[...] = jnp.zeros_like(acc_ref)
    for p in range(pairs):
        qp = q_ref[0, :, p * LANES:(p + 1) * LANES] * (LOG2E * SB_HEAD_DIM ** -0.5)
        qm_ref[p, :tq] = jnp.where(first, qp, 0.0).astype(BF16)
        qm_ref[p, tq:] = jnp.where(first, 0.0, qp).astype(BF16)

    def tile(j, masked):
        start = pl.multiple_of(j * tk, tk)
        z2s, sps = [], []
        for p in range(pairs):
            kb = k_ref[0, pl.ds(start, tk), p * LANES:(p + 1) * LANES]
            s = lax.dot_general(qm_ref[p], kb, (((1,), (1,)), ((), ())), preferred_element_type=F32)
            z2 = jnp.concatenate([s[:tq] + bias_ref[2 * p] * LOG2E, s[tq:] + bias_ref[2 * p + 1] * LOG2E], axis=0)
            neg_abs = lax.bitcast_convert_type(
                lax.bitcast_convert_type(z2, jnp.uint32) | jnp.uint32(0x80000000), F32)
            sp = jnp.maximum(z2, 0.0) + jnp.log2(1.0 + jnp.exp2(neg_abs))
            if masked:
                sp = jnp.where(causal, sp, 0.0)
            z2s.append(z2)
            sps.append(sp)
        incl_all = jnp.dot(jnp.concatenate(sps, axis=0).astype(BF16), tri, preferred_element_type=F32)
        for p in range(pairs):
            vb = v_ref[0, pl.ds(start, tk), p * LANES:(p + 1) * LANES]
            incl = incl_all[p * 2 * tq:(p + 1) * 2 * tq]
            c = c_ref[p]
            a = jnp.exp2(z2s[p] - incl - jnp.concatenate([c] * (tk // LANES), axis=1))
            if masked:
                a = jnp.where(causal, a, 0.0)
            acc_ref[p] += jnp.dot(a.astype(BF16), vb, preferred_element_type=F32)
            c_ref[p] = c + jnp.broadcast_to(incl[:, 0:1], (2 * tq, LANES))

    tile(i, True)

    def body(jj, carry):
        tile(i - 1 - jj, False)
        return carry

    lax.fori_loop(0, i, body, 0)
    for p in range(pairs):
        cols = slice(p * LANES, (p + 1) * LANES)
        o = jnp.where(first, acc_ref[p, :tq], acc_ref[p, tq:])
        o_ref[0, :, cols] = o * _silu(z_ref[0, :, cols])


SB_PROMPT_TILE = 256


def _sb_prompt(p3, pb3, bias):
    b, t, _ = p3.shape
    tq = min(SB_PROMPT_TILE, t)
    assert t % tq == 0 and tq % LANES == 0
    return pl.pallas_call(
        _sb_prompt_kernel,
        out_shape=jax.ShapeDtypeStruct((b, t, SB_WIDTH), F32),
        grid=(b, t // tq),
        in_specs=[pl.BlockSpec(memory_space=pltpu.SMEM),
                  pl.BlockSpec((1, tq, SB_WIDTH), lambda bi, i: (bi, i, 0)),
                  pl.BlockSpec((1, t, SB_WIDTH), lambda bi, i: (bi, 0, 1)),
                  pl.BlockSpec((1, t, SB_WIDTH), lambda bi, i: (bi, 0, 2)),
                  pl.BlockSpec((1, tq, SB_WIDTH), lambda bi, i: (bi, i, 3)),
                  pl.BlockSpec((tq, tq), lambda bi, i: (0, 0))],
        out_specs=pl.BlockSpec((1, tq, SB_WIDTH), lambda bi, i: (bi, i, 0)),
        scratch_shapes=[pltpu.VMEM((SB_HEADS // 2, 2 * tq, LANES), F32),
                        pltpu.VMEM((SB_HEADS // 2, 2 * tq, LANES), F32),
                        pltpu.VMEM((SB_HEADS // 2, 2 * tq, LANES), BF16)],
        compiler_params=_cparams("parallel", "arbitrary"),
        name="sb_prompt",
    )(bias, p3, pb3, pb3, p3, _suffix_matrix(tq))


SB_PAGES_PER_STEP = 16


def _tree_sum(xs):
    while len(xs) > 1:
        xs = [xs[k] + xs[k + 1] for k in range(0, len(xs) - 1, 2)] + ([xs[-1]] if len(xs) % 2 else [])
    return xs[0]


def _fold_sublanes(tiles):
    assert len(tiles) == SUBLANES
    order = [0, 4, 2, 6, 1, 5, 3, 7]
    cur = [tiles[k] for k in order]
    sub = lax.broadcasted_iota(jnp.int32, cur[0].shape, 0)
    s = SUBLANES // 2
    while s >= 1:
        low = (sub // s) % 2 == 0
        nxt = []
        for k in range(0, len(cur), 2):
            a, b = cur[k], cur[k + 1]
            fa = a + pltpu.roll(a, SUBLANES - s, axis=0)
            fb = b + pltpu.roll(b, s, axis=0)
            nxt.append(jnp.where(low, fa, fb))
        cur = nxt
        s //= 2
    return cur[0]


def _sb_decode_kernel(pt_ref, qb_ref, z_ref, bias_ref, tri_ref, *rest):
    del pt_ref
    g = SB_PAGES_PER_STEP
    k_refs = rest[:g]
    v_refs = rest[g:2 * g]
    o_ref = rest[2 * g]
    c_ref, acc_ref, a_ref, qs_ref = rest[2 * g + 1:]
    step = pl.program_id(1)
    page = k_refs[0].shape[3]

    @pl.when(step == 0)
    def _():
        c_ref[...] = jnp.zeros_like(c_ref)
        acc_ref[...] = jnp.zeros_like(acc_ref)
        qs_ref[...] = qb_ref[0] * (SB_HEAD_DIM ** -0.5)

    parts = [[None] * SB_HEADS for _ in range(g)]
    for h in range(SB_HEADS):
        rows = slice(h * SB_HEAD_DIM, (h + 1) * SB_HEAD_DIM)
        qh = qs_ref[rows, :]
        for i in range(g):
            prod = k_refs[i][0, 0, rows, :] * qh
            parts[i][h] = _tree_sum([prod[r:r + SUBLANES] for r in range(0, SB_HEAD_DIM, SUBLANES)])
    bias = bias_ref[...]
    zs = [_fold_sublanes(parts[i]) + bias for i in range(g)]
    lk = jnp.concatenate([_log_keep(z) for z in zs], axis=0)
    hi, lo = _split_bf16(lk)
    tri = tri_ref[...]
    r = jnp.dot(hi, tri, preferred_element_type=F32) + jnp.dot(lo, tri, preferred_element_type=F32)
    c = c_ref[...]
    for i in range(g):
        rows = slice(i * SB_HEADS, (i + 1) * SB_HEADS)
        a_ref[i] = jnp.exp(zs[i] + r[rows, :page] + c)
        c = c + r[rows, page:]
    c_ref[...] = c
    for h in range(SB_HEADS):
        rows = slice(h * SB_HEAD_DIM, (h + 1) * SB_HEAD_DIM)
        acc = acc_ref[rows, :]
        for i in range(g):
            acc = acc + v_refs[i][0, 0, rows, :] * jnp.broadcast_to(a_ref[i, h:h + 1, :], (SB_HEAD_DIM, page))
        acc_ref[rows, :] = acc

    @pl.when(step == pl.num_programs(1) - 1)
    def _():
        hi, lo = _split_bf16(acc_ref[...])
        ones = jnp.ones((SUBLANES, page), BF16)
        nt = (((1,), (1,)), ((), ()))
        o = (lax.dot_general(ones, hi, nt, preferred_element_type=F32)
             + lax.dot_general(ones, lo, nt, preferred_element_type=F32))
        rid = lax.broadcasted_iota(jnp.int32, o.shape, 0)
        o_ref[0] = jnp.where(rid == 0, o, 0.0) * _silu(z_ref[0])


def _sb_decode(p3, bias, cache_kt, cache_vt, layer, page_table):
    b, rows, _ = p3.shape
    n_pages = page_table.shape[1]
    page = cache_kt.shape[3]
    g = SB_PAGES_PER_STEP
    assert n_pages % g == 0 and page == LANES and rows == SUBLANES
    steps = n_pages // g

    def page_spec(i):
        return pl.BlockSpec((1, 1, SB_WIDTH, page),
                            lambda bi, s, pt: (layer, pt[bi, n_pages - 1 - (s * g + i)], 0, 0))

    bias_b = jnp.broadcast_to(bias.astype(F32)[:, None], (SB_HEADS, page))
    tri = jnp.concatenate([_suffix_matrix(page), jnp.ones((page, page), BF16)], axis=1)
    q_b =jnp.broadcast_to(p3[:, 0, :SB_WIDTH, None], (b, SB_WIDTH, page))
    grid_spec = pltpu.PrefetchScalarGridSpec(
        num_scalar_prefetch=1,
        grid=(b, steps),
        in_specs=[pl.BlockSpec((1, SB_WIDTH, page), lambda bi, s, pt: (bi, 0, 0)),
                  pl.BlockSpec((1, rows, SB_WIDTH), lambda bi, s, pt: (bi, 0, 3)),
                  pl.BlockSpec((SB_HEADS, page), lambda bi, s, pt: (0, 0)),
                  pl.BlockSpec((page, 2 * page), lambda bi, s, pt: (0, 0))]
        + [page_spec(i) for i in range(g)] + [page_spec(i) for i in range(g)],
        out_specs=pl.BlockSpec((1, rows, SB_WIDTH), lambda bi, s, pt: (bi, 0, 0)),
        scratch_shapes=[pltpu.VMEM((SB_HEADS, page), F32), pltpu.VMEM((SB_WIDTH, page), F32),
                        pltpu.VMEM((g, SB_HEADS, page), F32), pltpu.VMEM((SB_WIDTH, page), F32)],
    )
    return pl.pallas_call(
        _sb_decode_kernel,
        out_shape=jax.ShapeDtypeStruct((b, rows, SB_WIDTH), F32),
        grid_spec=grid_spec,
        compiler_params=_cparams("parallel", "arbitrary"),
        name="sb_decode",
    )(page_table, q_b, p3, bias_b, tri, *([cache_kt] * g), *([cache_vt] * g))


def _gdn_kernel(hp_ref, q_ref, k_ref, v_ref, gz_ref, gate_ref, cw_ref, ng_ref, buf_ref, s0_ref,
                o_ref, sout_ref, s_ref, tail_ref, *, t_true):
    n = pl.program_id(1)
    bb, c = q_ref.shape[0], q_ref.shape[1]
    nh = GDN_HEADS
    hc = nh * c

    @pl.when(n == 0)
    def _():
        tail_ref[...] = buf_ref[...]
        for bi in range(bb):
            for h in range(nh):
                s_ref[bi, :, h * GDN_DV:(h + 1) * GDN_DV] = s0_ref[bi, h]

    rowid = lax.broadcasted_iota(jnp.int32, (c, 1), 0) + n * c
    valid = (rowid < t_true).astype(F32)
    ri = lax.broadcasted_iota(jnp.int32, (hc, hc), 0)
    ci = lax.broadcasted_iota(jnp.int32, (hc, hc), 1)
    start = ri // c * c
    in_head = jnp.logical_and(ci >= start, ci < start + c)
    consts = dict(
        valid=valid, cw=cw_ref[...],
        lower_incl=jnp.logical_and(in_head, ci <= ri),
        lower_strict=jnp.logical_and(in_head, ci < ri),
        upper_incl=jnp.logical_and(in_head, ci >= ri),
        eye=(ri == ci).astype(F32),
        ones8=jnp.ones((SUBLANES, hc), BF16),
        own_block=(lax.broadcasted_iota(jnp.int32, (hc, nh * GDN_DV), 0) // c
                   == lax.broadcasted_iota(jnp.int32, (hc, nh * GDN_DV), 1) // GDN_DV))
    chains = [_gdn_chunk(bi, hp_ref, q_ref, k_ref, v_ref, gz_ref, gate_ref, ng_ref, o_ref, s_ref, tail_ref, consts)
              for bi in range(bb)]
    for _ in zip(*chains):
        pass
    for chain in chains:
        for _ in chain:
            pass

    @pl.when(n == pl.num_programs(1) - 1)
    def _():
        for bi in range(bb):
            for h in range(nh):
                sout_ref[bi, h] = s_ref[bi, :, h * GDN_DV:(h + 1) * GDN_DV]


def _gdn_chunk(bi, hp_ref, q_ref, k_ref, v_ref, gz_ref, gate_ref, ng_ref, o_ref, s_ref, tail_ref, k_):
    c = q_ref.shape[1]
    nh = GDN_HEADS
    hc = nh * c
    hw = nh * GDN_DK
    valid, cw = k_["valid"], k_["cw"]
    tail = tail_ref[bi]

    def conv(x_ref, sec):
        x = x_ref[bi]
        xx = jnp.concatenate([tail[:, sec * hw:(sec + 1) * hw], x], axis=0)
        w = cw[:, sec * hw:(sec + 1) * hw]
        y = w[CONV_W - 1:CONV_W] * x
        for i in range(CONV_W - 1):
            off = SUBLANES - (CONV_W - 1) + i
            y = y + w[i:i + 1] * xx[off:off + c]
        tail_ref[bi, :, sec * hw:(sec + 1) * hw] = x[c - SUBLANES:, :]
        return _silu(y) * valid

    def stack(x):
        return jnp.concatenate([x[:, h * GDN_DK:(h + 1) * GDN_DK] for h in range(nh)], axis=0)

    def own(x):
        return jnp.concatenate([x[h * c:(h + 1) * c, h * GDN_DV:(h + 1) * GDN_DV] for h in range(nh)], axis=0)

    def l2n(x):
        return x * lax.rsqrt(jnp.sum(x * x, axis=-1, keepdims=True) + NORM_EPS)

    q_s = l2n(stack(conv(q_ref, 0))) * (GDN_DK ** -0.5)
    k_s = l2n(stack(conv(k_ref, 1)))
    v_s = stack(conv(v_ref, 2))
    gates = gate_ref[bi]
    g_blk = -jnp.exp(hp_ref[0:1, :]) * _softplus(gates + hp_ref[1:2, :]) * valid
    b_blk = jax.nn.sigmoid(gates) * valid
    wide = max(hc, LANES)
    g_w = jnp.concatenate([jnp.broadcast_to(g_blk[:, h:h + 1], (c, wide)) for h in range(nh)], axis=0)
    beta = jnp.concatenate([jnp.broadcast_to(b_blk[:, nh + h:nh + h + 1], (c, LANES)) for h in range(nh)], axis=0)
    g_hi, g_lo = _split_bf16(g_w)
    tri = k_["lower_incl"].astype(BF16)
    gc_w = jnp.dot(tri, g_hi, preferred_element_type=F32) + jnp.dot(tri, g_lo, preferred_element_type=F32)
    u_hi, u_lo = _split_bf16(jnp.where(k_["upper_incl"], g_w[:, :hc], 0.0))
    grow = (jnp.dot(k_["ones8"], u_hi, preferred_element_type=F32)
            + jnp.dot(k_["ones8"], u_lo, preferred_element_type=F32))[0:1, :]
    yield
    decay = jnp.where(k_["lower_incl"], jnp.exp(gc_w[:, :hc] - grow), 0.0)
    gc = gc_w[:, :LANES]
    e_gc = jnp.exp(gc)
    g_last = jnp.concatenate([jnp.broadcast_to(gc[(h + 1) * c - 1:(h + 1) * c], (c, LANES)) for h in range(nh)],
                             axis=0)
    kb = k_s * beta
    aq = _dot_nt(jnp.concatenate([kb, q_s], axis=0), k_s)
    yield
    m = jnp.where(k_["lower_strict"], aq[:hc] * decay, 0.0)
    attn = aq[hc:] * decay
    pw = -m
    tinv = k_["eye"] + pw
    levels = c.bit_length() - 1
    if levels >= 2:
        pw = _dot(pw, pw)
        yield
    for r in range(1, levels):
        if r < levels - 1:
            both = _dot(jnp.concatenate([tinv, pw], axis=0), pw)
            yield
            tinv = tinv + both[:hc]
            pw = both[hc:]
        else:
            tinv = tinv + _dot(tinv, pw)
            yield
    uw = _dot(tinv, jnp.concatenate([v_s * beta, kb * e_gc], axis=1))
    yield
    s_old = s_ref[bi]
    ws = _dot(jnp.concatenate([uw[:, GDN_DV:], q_s * e_gc], axis=0), s_old)
    yield
    v_new = uw[:, :GDN_DV] - own(ws[:hc])
    o = own(ws[hc:]) + _dot(attn, v_new)
    yield
    kd = k_s * jnp.exp(g_last - gc)
    v_bd = jnp.where(k_["own_block"], jnp.concatenate([v_new] * nh, axis=1), 0.0)
    e_last = jnp.concatenate([jnp.exp(gc[(h + 1) * c - 1:(h + 1) * c]) for h in range(nh)], axis=1)
    s_ref[bi] = s_old * e_last + _dot_tn(kd, v_bd)
    o = _rms(o, ng_ref[...]) * _silu(stack(gz_ref[bi]))
    for h in range(nh):
        o_ref[bi, :, h * GDN_DV:(h + 1) * GDN_DV] = o[h * c:(h + 1) * c]


GDN_SEQS_PER_STEP = 4


def _gdn(p3, conv_w, a_log, dt_bias, norm_g, buf8, s0, t_true):
    b, tp, _ = p3.shape
    c = min(GDN_CHUNK, tp)
    bb = math.gcd(b, GDN_SEQS_PER_STEP)
    hw = GDN_HEADS * GDN_DK
    base = 4 * SB_WIDTH // hw
    kern = functools.partial(_gdn_kernel, t_true=t_true)
    head_params = jnp.pad(jnp.stack([a_log, dt_bias]).astype(F32), ((0, 0), (0, LANES - GDN_HEADS)))
    return pl.pallas_call(
        kern,
        out_shape=(jax.ShapeDtypeStruct((b, tp, GDN_WIDTH), F32),
                   jax.ShapeDtypeStruct((b, GDN_HEADS, GDN_DK, GDN_DV), F32)),
        grid=(b // bb, tp // c),
        in_specs=[pl.BlockSpec((2, LANES), lambda bi, n: (0, 0)),
                  pl.BlockSpec((bb, c, hw), lambda bi, n: (bi, n, base)),
                  pl.BlockSpec((bb, c, hw), lambda bi, n: (bi, n, base + 1)),
                  pl.BlockSpec((bb, c, hw), lambda bi, n: (bi, n, base + 2)),
                  pl.BlockSpec((bb, c, GDN_WIDTH), lambda bi, n: (bi, n, base + 3)),
                  pl.BlockSpec((bb, c, LANES), lambda bi, n: (bi, n, (4 * SB_WIDTH + GDN_CONV_DIM + GDN_WIDTH) // LANES)),
                  pl.BlockSpec((CONV_W, GDN_CONV_DIM), lambda bi, n: (0, 0)),
                  pl.BlockSpec((1, GDN_DV), lambda bi, n: (0, 0)),
                  pl.BlockSpec((bb, SUBLANES, GDN_CONV_DIM), lambda bi, n: (bi, 0, 0)),
                  pl.BlockSpec((bb, GDN_HEADS, GDN_DK, GDN_DV), lambda bi, n: (bi, 0, 0, 0))],
        out_specs=(pl.BlockSpec((bb, c, GDN_WIDTH), lambda bi, n: (bi, n, 0)),
                   pl.BlockSpec((bb, GDN_HEADS, GDN_DK, GDN_DV), lambda bi, n: (bi, 0, 0, 0))),
        scratch_shapes=[pltpu.VMEM((bb, GDN_DK, GDN_HEADS * GDN_DV), F32),
                        pltpu.VMEM((bb, SUBLANES, GDN_CONV_DIM), F32)],
        compiler_params=_cparams("parallel", "arbitrary"),
        name="gdn",
    )(head_params, p3, p3, p3, p3, p3, conv_w, norm_g.reshape(1, GDN_DV), buf8, s0)


def _rope_kernel(cos_ref, sin_ref, *, past_len):
    rows, half = cos_ref.shape
    rid = lax.broadcasted_iota(jnp.int32, (rows, 1), 0)
    pos = (rid + (pl.program_id(0) * rows + past_len)).astype(F32)
    fidx = lax.broadcasted_iota(jnp.int32, (1, half), 1).astype(F32)
    inv = jnp.exp(fidx * (-math.log(ROPE_BASE) / half))
    ang = pos * inv
    cos_ref[...] = jnp.cos(ang)
    sin_ref[...] = jnp.sin(ang)


def _rope_tables(tp, half, past_len):
    rows = _pick_tile(tp, 512, SUBLANES)
    shape = jax.ShapeDtypeStruct((tp, half), F32)
    spec = pl.BlockSpec((rows, half), lambda i: (i, 0))
    return pl.pallas_call(functools.partial(_rope_kernel, past_len=past_len), out_shape=(shape, shape),
                          grid=(tp // rows,), out_specs=(spec, spec), name="rope")()


def _ret_kernel(q_ref, k_ref, v_ref, z_ref, ng_ref, s0_ref, cos_ref, sin_ref, o_ref, sout_ref, s_ref, *,
                t_true, c_true):
    n = pl.program_id(1)
    c = q_ref.shape[1]
    dk = q_ref.shape[2] // RET_HEADS
    dv = v_ref.shape[2] // RET_HEADS
    half = dk // 2

    @pl.when(n == 0)
    def _():
        s_ref[...] = s0_ref[0]

    rid = lax.broadcasted_iota(jnp.int32, (c, 1), 0)
    valid = ((rid + n * c) < t_true).astype(F32)
    cos, sin = cos_ref[...], sin_ref[...]
    ii = lax.broadcasted_iota(jnp.int32, (c, c), 0)
    jj = lax.broadcasted_iota(jnp.int32, (c, c), 1)
    diff = (ii - jj).astype(F32)
    idx = rid.astype(F32)

    def rot(x):
        x1, x2 = x[:, :half], x[:, half:]
        return jnp.concatenate([x1 * cos - x2 * sin, x2 * cos + x1 * sin], axis=-1)

    def head(h):
        lg = math.log1p(-(2.0 ** (-5.0 - h)))
        qh = rot(q_ref[0, :, h * dk:(h + 1) * dk]) * valid
        kh = rot(k_ref[0, :, h * dk:(h + 1) * dk]) * (valid * (dk ** -0.5))
        vh = v_ref[0, :, h * dv:(h + 1) * dv] * valid
        dmask = jnp.where(ii >= jj, jnp.exp(lg * jnp.maximum(diff, 0.0)), 0.0)
        xi = jnp.exp(lg * (idx + 1.0))
        zeta = jnp.exp(lg * (c_true - 1.0 - idx))
        s_old = s_ref[h]
        inner = _dot_nt(qh, kh)
        yield
        o_state = _dot(qh * xi, s_old)
        yield
        o = _dot(inner * dmask, vh) + o_state
        yield
        s_ref[h] = math.exp(lg * c_true) * s_old + _dot_tn(kh * zeta, vh)
        yield
        o = _rms(o, ng_ref[...]) * _silu(z_ref[0, :, h * dv:(h + 1) * dv])
        o_ref[0, :, h * dv:(h + 1) * dv] = o

    chains = [head(h) for h in range(RET_HEADS)]
    for _ in zip(*chains):
        pass
    for chain in chains:
        for _ in chain:
            pass

    @pl.when(n == pl.num_programs(1) - 1)
    def _():
        sout_ref[0] = s_ref[...]


def _retention(p3, norm_g, s0, t_true, past_len):
    b, tp, width = p3.shape
    d = width // 6
    dk, dv = d // RET_HEADS, 2 * d // RET_HEADS
    c = min(RET_CHUNK, tp)
    kern = functools.partial(_ret_kernel, t_true=t_true, c_true=min(RET_CHUNK, t_true))
    cos, sin = _rope_tables(tp, dk // 2, past_len)
    return pl.pallas_call(
        kern,
        out_shape=(jax.ShapeDtypeStruct((b, tp, 2 * d), F32),
                   jax.ShapeDtypeStruct((b, RET_HEADS, dk, dv), F32)),
        grid=(b, tp // c),
        in_specs=[pl.BlockSpec((1, c, d), lambda bi, n: (bi, n, 0)),
                  pl.BlockSpec((1, c, d), lambda bi, n: (bi, n, 1)),
                  pl.BlockSpec((1, c, 2 * d), lambda bi, n: (bi, n, 1)),
                  pl.BlockSpec((1, c, 2 * d), lambda bi, n: (bi, n, 2)),
                  pl.BlockSpec((1, dv), lambda bi, n: (0, 0)),
                  pl.BlockSpec((1, RET_HEADS, dk, dv), lambda bi, n: (bi, 0, 0, 0)),
                  pl.BlockSpec((c, dk // 2), lambda bi, n: (n, 0)),
                  pl.BlockSpec((c, dk // 2), lambda bi, n: (n, 0))],
        out_specs=(pl.BlockSpec((1, c, 2 * d), lambda bi, n: (bi, n, 0)),
                   pl.BlockSpec((1, RET_HEADS, dk, dv), lambda bi, n: (bi, 0, 0, 0))),
        scratch_shapes=[pltpu.VMEM((RET_HEADS, dk, dv), F32)],
        compiler_params=_cparams("parallel", "arbitrary"),
        name="retention",
    )(p3, p3, p3, p3, norm_g.reshape(1, dv), s0, cos, sin)


def _post_kernel(*refs, n_mix, final, k_row, v_row):
    x_ref = refs[0]
    mix_refs = refs[1:1 + n_mix]
    w_refs = refs[1 + n_mix:1 + 2 * n_mix]
    gxa_ref, wq_ref, mk_ref, mv_ref, wo_ref, gf_ref, o_ref = refs[1 + 2 * n_mix:]
    bs, tm, d = x_ref.shape
    rows = bs * tm
    y = x_ref[...].reshape(rows, d)
    for m_ref, w_ref in zip(mix_refs, w_refs):
        y = y + jnp.dot(m_ref[...].reshape(rows, m_ref.shape[2]).astype(BF16), w_ref[...],
                        preferred_element_type=F32)
    h = _rms(y, gxa_ref[...])
    q = jnp.dot(h.astype(BF16), wq_ref[...], preferred_element_type=F32)
    hd = d // XA_HEADS
    out = [[None] * XA_HEADS for _ in range(bs)]

    def attend(s, i):
        qs = q[s * tm:(s + 1) * tm, i * hd:(i + 1) * hd]
        sc = _dot_nt(qs, mk_ref[s, k_row + i]) * (hd ** -0.5)
        yield
        sc = sc - jnp.max(sc, axis=-1, keepdims=True)
        p = jnp.exp(sc)
        p = p / jnp.sum(p, axis=-1, keepdims=True)
        out[s][i] = _dot(p, mv_ref[s, v_row + i])
        yield

    chains = [attend(s, i) for s in range(bs) for i in range(XA_HEADS)]
    for _ in zip(*chains):
        pass
    o = jnp.concatenate([jnp.concatenate(out[s], axis=-1) for s in range(bs)], axis=0)
    y = y + jnp.dot(o.astype(BF16), wo_ref[...], preferred_element_type=F32)
    if final:
        y = _rms(y, gf_ref[...])
    o_ref[...] = y.reshape(bs, tm, d)


POST_ROWS = 256
POST_SEQS_SHORT = 4


def _post(x, mixes, g_xa, wq, mem_k, mem_v, wo, g_final, final):
    b, tp, d = x.shape
    tm = min(tp, POST_ROWS)
    bs = math.gcd(b, POST_SEQS_SHORT) if tp == SUBLANES else 1
    (mk, k_lead, k_row), (mv, v_lead, v_row) = mem_k, mem_v
    row = lambda bi, i: (bi, i, 0)
    const = lambda bi, i: (0, 0)
    in_specs = [pl.BlockSpec((bs, tm, d), row)]
    in_specs += [pl.BlockSpec((bs, tm, a.shape[2]), row) for a, _ in mixes]
    in_specs += [pl.BlockSpec(w.shape, const) for _, w in mixes]
    in_specs += [pl.BlockSpec((1, d), const), pl.BlockSpec((d, d), const),
                 pl.BlockSpec((None, bs) + mk.shape[2:], lambda bi, i: (k_lead, bi, 0, 0, 0)),
                 pl.BlockSpec((None, bs) + mv.shape[2:], lambda bi, i: (v_lead, bi, 0, 0, 0)),
                 pl.BlockSpec((d, d), const), pl.BlockSpec((1, d), const)]
    mem_k, mem_v = mk, mv
    return pl.pallas_call(
        functools.partial(_post_kernel, n_mix=len(mixes), final=final, k_row=k_row, v_row=v_row),
        out_shape=jax.ShapeDtypeStruct((b, tp, d), F32),
        grid=(b // bs, tp // tm),
        in_specs=in_specs,
        out_specs=pl.BlockSpec((bs, tm, d), row),
        compiler_params=_cparams("parallel", "arbitrary"),
        name="post",
    )(x, *[a for a, _ in mixes], *[w for _, w in mixes], g_xa.reshape(1, d), wq, mem_k, mem_v, wo,
      g_final.reshape(1, d))


def _prep_weights(w_in_ab, w_out_ab, w_in_c, w_out_c, xa_wq, xa_wo):
    n_ab_pad = -(-AB_IN // LANES) * LANES
    return dict(
        w_in_ab=jnp.pad(w_in_ab, ((0, 0), (0, 0), (0, n_ab_pad - AB_IN))).astype(BF16),
        w_out_ab=w_out_ab.astype(BF16), w_in_c=w_in_c.astype(BF16), w_out_c=w_out_c.astype(BF16),
        xa_wq=xa_wq.astype(BF16), xa_wo=xa_wo.astype(BF16))


def _trunk(x, past_len, sb_fn, conv_bufs, gdn_states, ret_states, mem_k, mem_v, wts,
           norm_mix_g, norm_xa_g, sb_logit_bias, conv_w_gdn, gdn_a_log, gdn_dt_bias, gdn_norm_g,
           ret_norm_g, norm_final_g):
    b, t, d = x.shape
    tp = -(-t // SUBLANES) * SUBLANES
    if tp != t:
        x = jnp.pad(x, ((0, 0), (0, tp - t), (0, 0)))
    depth = norm_mix_g.shape[0]
    ks, vs, convs, gdns, rets = [], [], [], [], []
    for layer in range(depth):
        i = layer // 2
        last = layer == depth - 1
        if layer % 2 == 0:
            p, pb = _proj(x.reshape(b * tp, d), norm_mix_g[layer], wts["w_in_ab"][i], norm=True, emit_bf16=True)
            p = p.reshape(b, tp, -1)
            o_sb = sb_fn(i, p, pb.reshape(b, tp, -1))
            buf8 = jnp.pad(conv_bufs[i], ((0, 0), (SUBLANES - (CONV_W - 1), 0), (0, 0)))
            o_g, s_gdn = _gdn(p, conv_w_gdn[i], gdn_a_log[i], gdn_dt_bias[i], gdn_norm_g[i], buf8, gdn_states[i], t)
            ks.append(p[:, :t, SB_WIDTH:2 * SB_WIDTH].reshape(b, t, SB_HEADS, SB_HEAD_DIM))
            vs.append(p[:, :t, 2 * SB_WIDTH:3 * SB_WIDTH].reshape(b, t, SB_HEADS, SB_HEAD_DIM))
            qkv = p[:, :t, 4 * SB_WIDTH:4 * SB_WIDTH + GDN_CONV_DIM]
            keep = CONV_W - 1
            convs.append(jnp.concatenate([conv_bufs[i], qkv[:, max(t - keep, 0):]], axis=1)[:, -keep:])
            gdns.append(s_gdn)
            mixes = [(o_sb, wts["w_out_ab"][i, :SB_WIDTH]), (o_g, wts["w_out_ab"][i, SB_WIDTH:])]
        else:
            p = _proj(x.reshape(b * tp, d), norm_mix_g[layer], wts["w_in_c"][i], norm=True).reshape(b, tp, -1)
            o_r, s_ret = _retention(p, ret_norm_g[i], ret_states[i], t, past_len)
            rets.append(s_ret)
            mixes = [(o_r, wts["w_out_c"][i])]
        x = _post(x, mixes, norm_xa_g[layer], wts["xa_wq"][layer], mem_k(layer), mem_v(layer),
                  wts["xa_wo"][layer], norm_final_g, last)
    return (x[:, :t], jnp.stack(ks), jnp.stack(vs), jnp.stack(gdns), jnp.stack(convs), jnp.stack(rets))


def kernel(x_prompt, x_sample, mem_prompt, cache_sb_k, cache_sb_v, state_gdn, state_gdn_conv, state_ret,
           cache_mem_k, cache_mem_v, page_table, norm_mix_g, norm_xa_g, w_in_ab, sb_logit_bias, conv_w_gdn,
           gdn_a_log, gdn_dt_bias, gdn_norm_g, w_out_ab, w_in_c, ret_norm_g, w_out_c, xa_wq, xa_wk, xa_wv,
           xa_wo, norm_final_g):
    wts = _prep_weights(w_in_ab, w_out_ab, w_in_c, w_out_c, xa_wq, xa_wo)
    shared = (norm_mix_g, norm_xa_g, sb_logit_bias, conv_w_gdn, gdn_a_log, gdn_dt_bias, gdn_norm_g,
              ret_norm_g, norm_final_g)
    depth = norm_mix_g.shape[0]
    n_ab, n_c = state_gdn.shape[0], state_ret.shape[0]

    bp, mlen, d = mem_prompt.shape
    hd = d // XA_HEADS
    w_kv = jnp.concatenate([xa_wk, xa_wv], axis=0).astype(BF16)
    w_kv = jnp.moveaxis(w_kv, 0, 1).reshape(d, 2 * depth * d)
    mem_kv = _proj(mem_prompt.reshape(bp * mlen, d), jnp.ones((d,), F32), w_kv, norm=False)
    mem_kv = mem_kv.reshape(bp, mlen, 2 * depth, XA_HEADS, hd)
    mem_rows = jnp.transpose(mem_kv.reshape(bp, mlen, 2 * depth * XA_HEADS, hd), (0, 2, 1, 3))[None]
    mem_kv = jnp.moveaxis(mem_kv, 2, 0)
    mem_k_prompt, mem_v_prompt = mem_kv[:depth], mem_kv[depth:]
    conv0 = jnp.zeros((n_ab, bp) + state_gdn_conv.shape[2:], F32)
    gdn0 = jnp.zeros((n_ab, bp) + state_gdn.shape[2:], F32)
    ret0 = jnp.zeros((n_c, bp) + state_ret.shape[2:], F32)
    prompt = _trunk(x_prompt, 0, lambda i, p, pb: _sb_prompt(p, pb, sb_logit_bias[i]), conv0, gdn0, ret0,
                    lambda l: (mem_rows, 0, l * XA_HEADS), lambda l: (mem_rows, 0, (depth + l) * XA_HEADS),
                    wts, *shared)

    n_pages, page = page_table.shape[1], cache_sb_k.shape[2]
    assert x_sample.shape[1] == 1
    n_phys = cache_sb_k.shape[1]
    ckt = jnp.transpose(cache_sb_k, (0, 1, 3, 4, 2)).reshape(n_ab, n_phys, SB_WIDTH, page)
    cvt = jnp.transpose(cache_sb_v, (0, 1, 3, 4, 2)).reshape(n_ab, n_phys, SB_WIDTH, page)
    smem_k = jnp.transpose(cache_mem_k, (0, 1, 3, 2, 4))
    smem_v = jnp.transpose(cache_mem_v, (0, 1, 3, 2, 4))
    sample = _trunk(x_sample, n_pages * page,
                    lambda i, p, pb: _sb_decode(p, sb_logit_bias[i], ckt, cvt, i, page_table),
                    state_gdn_conv, state_gdn, state_ret,
                    lambda l: (smem_k, l, 0), lambda l: (smem_v, l, 0), wts, *shared)

    y_p, k_p, v_p, gdn_p, conv_p, ret_p = prompt
    y_s, k_s, v_s, gdn_s, conv_s, ret_s = sample
    return (y_p, y_s, k_p, v_p, gdn_p, conv_p, ret_p, mem_k_prompt, mem_v_prompt,
            k_s, v_s, gdn_s, conv_s, ret_s)
```

```python
import functools
import math

import numpy as np
import jax
import jax.numpy as jnp
from jax import lax
from jax.experimental import pallas as pl
from jax.experimental.pallas import tpu as pltpu

F32 = jnp.float32
BF16 = jnp.bfloat16

SB_HEADS = 8
SB_HEAD_DIM = 64
SB_WIDTH = SB_HEADS * SB_HEAD_DIM
SB_TILE = 128
GDN_HEADS = 4
GDN_DK = 128
GDN_DV = 128
GDN_WIDTH = GDN_HEADS * GDN_DV
GDN_CONV_DIM = GDN_HEADS * (2 * GDN_DK + GDN_DV)
CONV_W = 4
GDN_CHUNK = 64
AB_IN = 4 * SB_WIDTH + GDN_CONV_DIM + GDN_WIDTH + 2 * GDN_HEADS
RET_HEADS = 4
RET_CHUNK = 128
ROPE_BASE = 10000.0
XA_HEADS = 4
NORM_EPS = 1e-6
LANES = 128
SUBLANES = 8
VMEM_LIMIT = 56 * 1024 * 1024


def _cparams(*sem):
    return pltpu.CompilerParams(dimension_semantics=sem, vmem_limit_bytes=VMEM_LIMIT)


def _rms(x, g):
    ms = jnp.mean(x * x, axis=-1, keepdims=True)
    return x * lax.rsqrt(ms + NORM_EPS) * g


def _silu(x):
    return x * jax.nn.sigmoid(x)


def _softplus(x):
    return jnp.maximum(x, 0.0) + jnp.log1p(jnp.exp(-jnp.abs(x)))


def _dot(a, b):
    return jnp.dot(a.astype(BF16), b.astype(BF16), preferred_element_type=F32)


def _dot_nt(a, b):
    return lax.dot_general(a.astype(BF16), b.astype(BF16), (((1,), (1,)), ((), ())),
                           preferred_element_type=F32)


def _dot_tn(a, b):
    return lax.dot_general(a.astype(BF16), b.astype(BF16), (((0,), (0,)), ((), ())),
                           preferred_element_type=F32)


def _mask_rows(x, valid):
    return x if valid is None else x * valid


def _split_bf16(x):
    hi = x.astype(BF16)
    lo = (x - hi.astype(F32)).astype(BF16)
    return hi, lo


def _proj_kernel(x_ref, g_ref, w_ref, o_ref, *rest, norm, emit_bf16):
    h_ref = rest[-1]

    @pl.when(pl.program_id(1) == 0)
    def _():
        x = x_ref[...]
        if norm:
            x = _rms(x, g_ref[...])
        h_ref[...] = x.astype(BF16)

    res = jnp.dot(h_ref[...], w_ref[...], preferred_element_type=F32)
    o_ref[...] = res
    if emit_bf16:
        rest[0][...] = res.astype(BF16)


def _pick_tile(n, cap, unit):
    best = unit
    for t in range(unit, min(n, cap) + 1, unit):
        if n % t == 0:
            best = t
    return best


PROJ_ROWS = 1024


def _proj(x2d, g, w_bf16, *, norm, emit_bf16=False):
    m, d = x2d.shape
    n = w_bf16.shape[1]
    tm = m if m <= PROJ_ROWS else _pick_tile(m, PROJ_ROWS, SUBLANES)
    tn = _pick_tile(n, 2048, LANES)
    out_spec = pl.BlockSpec((tm, tn), lambda i, j: (i, j))
    out_shape = jax.ShapeDtypeStruct((m, n), F32)
    if emit_bf16:
        out_shape = (out_shape, jax.ShapeDtypeStruct((m, n), BF16))
        out_spec = (out_spec, pl.BlockSpec((tm, tn), lambda i, j: (i, j)))
    return pl.pallas_call(
        functools.partial(_proj_kernel, norm=norm, emit_bf16=emit_bf16),
        out_shape=out_shape,
        grid=(m // tm, n // tn),
        in_specs=[pl.BlockSpec((tm, d), lambda i, j: (i, 0)),
                  pl.BlockSpec((1, d), lambda i, j: (0, 0)),
                  pl.BlockSpec((d, tn), lambda i, j: (0, j))],
        out_specs=out_spec,
        scratch_shapes=[pltpu.VMEM((tm, d), BF16)],
        compiler_params=_cparams("parallel", "arbitrary"),
        name="proj",
    )(x2d, g.reshape(1, d), w_bf16)


def _suffix_matrix(n):
    j = np.arange(n)[:, None]
    s = np.arange(n)[None, :]
    return jnp.asarray((j >= s).astype(np.float32), dtype=BF16)


def _log_keep(z):
    return -(jnp.maximum(z, 0.0) + jnp.log(1.0 + jnp.exp(-jnp.abs(z))))


LOG2E = math.log2(math.e)


def _sb_prompt_kernel(bias_ref, q_ref, k_ref, v_ref, z_ref, tri_ref, o_ref, c_ref, acc_ref, qm_ref):
    i = pl.program_id(1)
    tq = q_ref.shape[1]
    tk = tq
    pairs = SB_HEADS // 2
    lane = lax.broadcasted_iota(jnp.int32, (tq, LANES), 1)
    first = lane < SB_HEAD_DIM
    row = lax.broadcasted_iota(jnp.int32, (2 * tq, tk), 0)
    col = lax.broadcasted_iota(jnp.int32, (2 * tq, tk), 1)
    causal = col < jnp.where(row >= tq, row - tq, row)
    tri = tri_ref[...]
    c_ref[...] = jnp.zeros_like(c_ref)
    acc_ref[...] = jnp.zeros_like(acc_ref)
    for p in range(pairs):
        qp = q_ref[0, :, p * LANES:(p + 1) * LANES] * (LOG2E * SB_HEAD_DIM ** -0.5)
        qm_ref[p, :tq] = jnp.where(first, qp, 0.0).astype(BF16)
        qm_ref[p, tq:] = jnp.where(first, 0.0, qp).astype(BF16)

    def tile(j, masked):
        start = pl.multiple_of(j * tk, tk)
        z2s, sps = [], []
        for p in range(pairs):
            kb = k_ref[0, pl.ds(start, tk), p * LANES:(p + 1) * LANES]
            s = lax.dot_general(qm_ref[p], kb, (((1,), (1,)), ((), ())), preferred_element_type=F32)
            z2 = jnp.concatenate([s[:tq] + bias_ref[2 * p] * LOG2E, s[tq:] + bias_ref[2 * p + 1] * LOG2E], axis=0)
            neg_abs = lax.bitcast_convert_type(
                lax.bitcast_convert_type(z2, jnp.uint32) | jnp.uint32(0x80000000), F32)
            sp = jnp.maximum(z2, 0.0) + jnp.log2(1.0 + jnp.exp2(neg_abs))
            if masked:
                sp = jnp.where(causal, sp, 0.0)
            z2s.append(z2)
            sps.append(sp)
        incl_all = jnp.dot(jnp.concatenate(sps, axis=0).astype(BF16), tri, preferred_element_type=F32)
        for p in range(pairs):
            vb = v_ref[0, pl.ds(start, tk), p * LANES:(p + 1) * LANES]
            incl = incl_all[p * 2 * tq:(p + 1) * 2 * tq]
            c = c_ref[p]
            a = jnp.exp2(z2s[p] - incl - jnp.concatenate([c] * (tk // LANES), axis=1))
            if masked:
                a = jnp.where(causal, a, 0.0)
            acc_ref[p] += jnp.dot(a.astype(BF16), vb, preferred_element_type=F32)
            c_ref[p] = c + jnp.broadcast_to(incl[:, 0:1], (2 * tq, LANES))

    tile(i, True)

    def body(jj, carry):
        tile(i - 1 - jj, False)
        return carry

    lax.fori_loop(0, i, body, 0)
    for p in range(pairs):
        cols = slice(p * LANES, (p + 1) * LANES)
        o = jnp.where(first, acc_ref[p, :tq], acc_ref[p, tq:])
        o_ref[0, :, cols] = o * _silu(z_ref[0, :, cols])


SB_PROMPT_TILE = 256


def _sb_prompt(p3, pb3, bias):
    b, t, _ = p3.shape
    tq = min(SB_PROMPT_TILE, t)
    assert t % tq == 0 and tq % LANES == 0
    return pl.pallas_call(
        _sb_prompt_kernel,
        out_shape=jax.ShapeDtypeStruct((b, t, SB_WIDTH), F32),
        grid=(b, t // tq),
        in_specs=[pl.BlockSpec(memory_space=pltpu.SMEM),
                  pl.BlockSpec((1, tq, SB_WIDTH), lambda bi, i: (bi, i, 0)),
                  pl.BlockSpec((1, t, SB_WIDTH), lambda bi, i: (bi, 0, 1)),
                  pl.BlockSpec((1, t, SB_WIDTH), lambda bi, i: (bi, 0, 2)),
                  pl.BlockSpec((1, tq, SB_WIDTH), lambda bi, i: (bi, i, 3)),
                  pl.BlockSpec((tq, tq), lambda bi, i: (0, 0))],
        out_specs=pl.BlockSpec((1, tq, SB_WIDTH), lambda bi, i: (bi, i, 0)),
        scratch_shapes=[pltpu.VMEM((SB_HEADS // 2, 2 * tq, LANES), F32),
                        pltpu.VMEM((SB_HEADS // 2, 2 * tq, LANES), F32),
                        pltpu.VMEM((SB_HEADS // 2, 2 * tq, LANES), BF16)],
        compiler_params=_cparams("parallel", "arbitrary"),
        name="sb_prompt",
    )(bias, p3, pb3, pb3, p3, _suffix_matrix(tq))


SB_PAGES_PER_STEP = 32


def _tree_sum(xs):
    while len(xs) > 1:
        xs = [xs[k] + xs[k + 1] for k in range(0, len(xs) - 1, 2)] + ([xs[-1]] if len(xs) % 2 else [])
    return xs[0]


def _fold_sublanes(tiles):
    assert len(tiles) == SUBLANES
    order = [0, 4, 2, 6, 1, 5, 3, 7]
    cur = [tiles[k] for k in order]
    sub = lax.broadcasted_iota(jnp.int32, cur[0].shape, 0)
    s = SUBLANES // 2
    while s >= 1:
        low = (sub // s) % 2 == 0
        nxt = []
        for k in range(0, len(cur), 2):
            a, b = cur[k], cur[k + 1]
            fa = a + pltpu.roll(a, SUBLANES - s, axis=0)
            fb = b + pltpu.roll(b, s, axis=0)
            nxt.append(jnp.where(low, fa, fb))
        cur = nxt
        s //= 2
    return cur[0]


def _sb_decode_kernel(pt_ref, qb_ref, z_ref, bias_ref, tri_ref, *rest):
    del pt_ref
    g = SB_PAGES_PER_STEP
    k_refs = rest[:g]
    v_refs = rest[g:2 * g]
    o_ref = rest[2 * g]
    c_ref, acc_ref, a_ref, qs_ref = rest[2 * g + 1:]
    step = pl.program_id(1)
    page = k_refs[0].shape[3]

    @pl.when(step == 0)
    def _():
        c_ref[...] = jnp.zeros_like(c_ref)
        acc_ref[...] = jnp.zeros_like(acc_ref)
        qs_ref[...] = qb_ref[0] * (SB_HEAD_DIM ** -0.5)

    parts = [[None] * SB_HEADS for _ in range(g)]
    for h in range(SB_HEADS):
        rows = slice(h * SB_HEAD_DIM, (h + 1) * SB_HEAD_DIM)
        qh = qs_ref[rows, :]
        for i in range(g):
            prod = k_refs[i][0, 0, rows, :] * qh
            parts[i][h] = _tree_sum([prod[r:r + SUBLANES] for r in range(0, SB_HEAD_DIM, SUBLANES)])
    bias = bias_ref[...]
    zs = [_fold_sublanes(parts[i]) + bias for i in range(g)]
    lk = jnp.concatenate([_log_keep(z) for z in zs], axis=0)
    hi, lo = _split_bf16(lk)
    tri = tri_ref[...]
    r = jnp.dot(hi, tri, preferred_element_type=F32) + jnp.dot(lo, tri, preferred_element_type=F32)
    c = c_ref[...]
    for i in range(g):
        rows = slice(i * SB_HEADS, (i + 1) * SB_HEADS)
        a_ref[i] = jnp.exp(zs[i] + r[rows, :page] + c)
        c = c + r[rows, page:]
    c_ref[...] = c
    for h in range(SB_HEADS):
        rows = slice(h * SB_HEAD_DIM, (h + 1) * SB_HEAD_DIM)
        acc = acc_ref[rows, :]
        for i in range(g):
            acc = acc + v_refs[i][0, 0, rows, :] * jnp.broadcast_to(a_ref[i, h:h + 1, :], (SB_HEAD_DIM, page))
        acc_ref[rows, :] = acc

    @pl.when(step == pl.num_programs(1) - 1)
    def _():
        hi, lo = _split_bf16(acc_ref[...])
        ones = jnp.ones((SUBLANES, page), BF16)
        nt = (((1,), (1,)), ((), ()))
        o = (lax.dot_general(ones, hi, nt, preferred_element_type=F32)
             + lax.dot_general(ones, lo, nt, preferred_element_type=F32))
        rid = lax.broadcasted_iota(jnp.int32, o.shape, 0)
        o_ref[0] = jnp.where(rid == 0, o, 0.0) * _silu(z_ref[0])


def _sb_decode(p3, bias, cache_kt, cache_vt, layer, page_table):
    b, rows, _ = p3.shape
    n_pages = page_table.shape[1]
    page = cache_kt.shape[3]
    g = SB_PAGES_PER_STEP
    assert n_pages % g == 0 and page == LANES and rows == SUBLANES
    steps = n_pages // g

    def page_spec(i):
        return pl.BlockSpec((1, 1, SB_WIDTH, page),
                            lambda bi, s, pt: (layer, pt[bi, n_pages - 1 - (s * g + i)], 0, 0))

    bias_b = jnp.broadcast_to(bias.astype(F32)[:, None], (SB_HEADS, page))
    tri = jnp.concatenate([_suffix_matrix(page), jnp.ones((page, page), BF16)], axis=1)
    q_b =jnp.broadcast_to(p3[:, 0, :SB_WIDTH, None], (b, SB_WIDTH, page))
    grid_spec = pltpu.PrefetchScalarGridSpec(
        num_scalar_prefetch=1,
        grid=(b, steps),
        in_specs=[pl.BlockSpec((1, SB_WIDTH, page), lambda bi, s, pt: (bi, 0, 0)),
                  pl.BlockSpec((1, rows, SB_WIDTH), lambda bi, s, pt: (bi, 0, 3)),
                  pl.BlockSpec((SB_HEADS, page), lambda bi, s, pt: (0, 0)),
                  pl.BlockSpec((page, 2 * page), lambda bi, s, pt: (0, 0))]
        + [page_spec(i) for i in range(g)] + [page_spec(i) for i in range(g)],
        out_specs=pl.BlockSpec((1, rows, SB_WIDTH), lambda bi, s, pt: (bi, 0, 0)),
        scratch_shapes=[pltpu.VMEM((SB_HEADS, page), F32), pltpu.VMEM((SB_WIDTH, page), F32),
                        pltpu.VMEM((g, SB_HEADS, page), F32), pltpu.VMEM((SB_WIDTH, page), F32)],
    )
    return pl.pallas_call(
        _sb_decode_kernel,
        out_shape=jax.ShapeDtypeStruct((b, rows, SB_WIDTH), F32),
        grid_spec=grid_spec,
        compiler_params=_cparams("parallel", "arbitrary"),
        name="sb_decode",
    )(page_table, q_b, p3, bias_b, tri, *([cache_kt] * g), *([cache_vt] * g))


def _gdn_kernel(hp_ref, q_ref, k_ref, v_ref, gz_ref, gate_ref, cw_ref, ng_ref, buf_ref, s0_ref,
                o_ref, sout_ref, s_ref, tail_ref, *, t_true, padded):
    n = pl.program_id(1)
    bb, c = q_ref.shape[0], q_ref.shape[1]
    nh = GDN_HEADS
    hc = nh * c

    @pl.when(n == 0)
    def _():
        tail_ref[...] = buf_ref[...]
        for bi in range(bb):
            for h in range(nh):
                s_ref[bi, :, h * GDN_DV:(h + 1) * GDN_DV] = s0_ref[bi, h]

    rowid = lax.broadcasted_iota(jnp.int32, (c, 1), 0) + n * c
    valid = (rowid < t_true).astype(F32) if padded else None
    ri = lax.broadcasted_iota(jnp.int32, (hc, hc), 0)
    ci = lax.broadcasted_iota(jnp.int32, (hc, hc), 1)
    start = ri // c * c
    in_head = jnp.logical_and(ci >= start, ci < start + c)
    consts = dict(
        valid=valid, cw=cw_ref[...],
        lower_incl=jnp.logical_and(in_head, ci <= ri),
        lower_strict=jnp.logical_and(in_head, ci < ri),
        upper_incl=jnp.logical_and(in_head, ci >= ri),
        eye=(ri == ci).astype(F32),
        ones8=jnp.ones((SUBLANES, hc), BF16),
        own_block=(lax.broadcasted_iota(jnp.int32, (hc, nh * GDN_DV), 0) // c
                   == lax.broadcasted_iota(jnp.int32, (hc, nh * GDN_DV), 1) // GDN_DV))
    chains = [_gdn_chunk(bi, hp_ref, q_ref, k_ref, v_ref, gz_ref, gate_ref, ng_ref, o_ref, s_ref, tail_ref, consts)
              for bi in range(bb)]
    for _ in zip(*chains):
        pass
    for chain in chains:
        for _ in chain:
            pass

    @pl.when(n == pl.num_programs(1) - 1)
    def _():
        for bi in range(bb):
            for h in range(nh):
                sout_ref[bi, h] = s_ref[bi, :, h * GDN_DV:(h + 1) * GDN_DV]


def _gdn_chunk(bi, hp_ref, q_ref, k_ref, v_ref, gz_ref, gate_ref, ng_ref, o_ref, s_ref, tail_ref, k_):
    c = q_ref.shape[1]
    nh = GDN_HEADS
    hc = nh * c
    hw = nh * GDN_DK
    valid, cw = k_["valid"], k_["cw"]
    tail = tail_ref[bi]

    def conv(x_ref, sec):
        x = x_ref[bi]
        xx = jnp.concatenate([tail[:, sec * hw:(sec + 1) * hw], x], axis=0)
        w = cw[:, sec * hw:(sec + 1) * hw]
        y = w[CONV_W - 1:CONV_W] * x
        for i in range(CONV_W - 1):
            off = SUBLANES - (CONV_W - 1) + i
            y = y + w[i:i + 1] * xx[off:off + c]
        tail_ref[bi, :, sec * hw:(sec + 1) * hw] = x[c - SUBLANES:, :]
        return _mask_rows(_silu(y), valid)

    def stack(x):
        return jnp.concatenate([x[:, h * GDN_DK:(h + 1) * GDN_DK] for h in range(nh)], axis=0)

    def own(x):
        return jnp.concatenate([x[h * c:(h + 1) * c, h * GDN_DV:(h + 1) * GDN_DV] for h in range(nh)], axis=0)

    def l2n(x):
        return x * lax.rsqrt(jnp.sum(x * x, axis=-1, keepdims=True) + NORM_EPS)

    q_s = l2n(stack(conv(q_ref, 0))) * (GDN_DK ** -0.5)
    k_s = l2n(stack(conv(k_ref, 1)))
    v_s = stack(conv(v_ref, 2))
    gates = gate_ref[bi]
    g_blk = _mask_rows(-jnp.exp(hp_ref[0:1, :]) * _softplus(gates + hp_ref[1:2, :]), valid)
    b_blk = _mask_rows(jax.nn.sigmoid(gates), valid)
    wide = max(hc, LANES)
    g_w = jnp.concatenate([jnp.broadcast_to(g_blk[:, h:h + 1], (c, wide)) for h in range(nh)], axis=0)
    beta = jnp.concatenate([jnp.broadcast_to(b_blk[:, nh + h:nh + h + 1], (c, LANES)) for h in range(nh)], axis=0)
    g_hi, g_lo = _split_bf16(g_w)
    tri = k_["lower_incl"].astype(BF16)
    gc_w = jnp.dot(tri, g_hi, preferred_element_type=F32) + jnp.dot(tri, g_lo, preferred_element_type=F32)
    u_hi, u_lo = _split_bf16(jnp.where(k_["upper_incl"], g_w[:, :hc], 0.0))
    grow = (jnp.dot(k_["ones8"], u_hi, preferred_element_type=F32)
            + jnp.dot(k_["ones8"], u_lo, preferred_element_type=F32))[0:1, :]
    yield
    decay = jnp.where(k_["lower_incl"], jnp.exp(gc_w[:, :hc] - grow), 0.0)
    gc = gc_w[:, :LANES]
    e_gc = jnp.exp(gc)
    g_last = jnp.concatenate([jnp.broadcast_to(gc[(h + 1) * c - 1:(h + 1) * c], (c, LANES)) for h in range(nh)],
                             axis=0)
    kb = k_s * beta
    aq = _dot_nt(jnp.concatenate([kb, q_s], axis=0), k_s)
    yield
    m = jnp.where(k_["lower_strict"], aq[:hc] * decay, 0.0)
    attn = aq[hc:] * decay
    pw = -m
    tinv = k_["eye"] + pw
    levels = c.bit_length() - 1
    if levels >= 2:
        pw = _dot(pw, pw)
        yield
    for r in range(1, levels):
        if r < levels - 1:
            both = _dot(jnp.concatenate([tinv, pw], axis=0), pw)
            yield
            tinv = tinv + both[:hc]
            pw = both[hc:]
        else:
            tinv = tinv + _dot(tinv, pw)
            yield
    uw = _dot(tinv, jnp.concatenate([v_s * beta, kb * e_gc], axis=1))
    yield
    s_old = s_ref[bi]
    ws = _dot(jnp.concatenate([uw[:, GDN_DV:], q_s * e_gc], axis=0), s_old)
    yield
    v_new = uw[:, :GDN_DV] - own(ws[:hc])
    o = own(ws[hc:]) + _dot(attn, v_new)
    yield
    kd = k_s * jnp.exp(g_last - gc)
    v_bd = jnp.where(k_["own_block"], jnp.concatenate([v_new] * nh, axis=1), 0.0)
    e_last = jnp.concatenate([jnp.exp(gc[(h + 1) * c - 1:(h + 1) * c]) for h in range(nh)], axis=1)
    s_ref[bi] = s_old * e_last + _dot_tn(kd, v_bd)
    o = _rms(o, ng_ref[...]) * _silu(stack(gz_ref[bi]))
    for h in range(nh):
        o_ref[bi, :, h * GDN_DV:(h + 1) * GDN_DV] = o[h * c:(h + 1) * c]


GDN_SEQS_PER_STEP = 4


def _gdn(p3, conv_w, a_log, dt_bias, norm_g, buf8, s0, t_true):
    b, tp, _ = p3.shape
    c = min(GDN_CHUNK, tp)
    bb = math.gcd(b, GDN_SEQS_PER_STEP)
    hw = GDN_HEADS * GDN_DK
    base = 4 * SB_WIDTH // hw
    kern = functools.partial(_gdn_kernel, t_true=t_true, padded=t_true != tp)
    head_params = jnp.pad(jnp.stack([a_log, dt_bias]).astype(F32), ((0, 0), (0, LANES - GDN_HEADS)))
    return pl.pallas_call(
        kern,
        out_shape=(jax.ShapeDtypeStruct((b, tp, GDN_WIDTH), F32),
                   jax.ShapeDtypeStruct((b, GDN_HEADS, GDN_DK, GDN_DV), F32)),
        grid=(b // bb, tp // c),
        in_specs=[pl.BlockSpec((2, LANES), lambda bi, n: (0, 0)),
                  pl.BlockSpec((bb, c, hw), lambda bi, n: (bi, n, base)),
                  pl.BlockSpec((bb, c, hw), lambda bi, n: (bi, n, base + 1)),
                  pl.BlockSpec((bb, c, hw), lambda bi, n: (bi, n, base + 2)),
                  pl.BlockSpec((bb, c, GDN_WIDTH), lambda bi, n: (bi, n, base + 3)),
                  pl.BlockSpec((bb, c, LANES), lambda bi, n: (bi, n, (4 * SB_WIDTH + GDN_CONV_DIM + GDN_WIDTH) // LANES)),
                  pl.BlockSpec((CONV_W, GDN_CONV_DIM), lambda bi, n: (0, 0)),
                  pl.BlockSpec((1, GDN_DV), lambda bi, n: (0, 0)),
                  pl.BlockSpec((bb, SUBLANES, GDN_CONV_DIM), lambda bi, n: (bi, 0, 0)),
                  pl.BlockSpec((bb, GDN_HEADS, GDN_DK, GDN_DV), lambda bi, n: (bi, 0, 0, 0))],
        out_specs=(pl.BlockSpec((bb, c, GDN_WIDTH), lambda bi, n: (bi, n, 0)),
                   pl.BlockSpec((bb, GDN_HEADS, GDN_DK, GDN_DV), lambda bi, n: (bi, 0, 0, 0))),
        scratch_shapes=[pltpu.VMEM((bb, GDN_DK, GDN_HEADS * GDN_DV), F32),
                        pltpu.VMEM((bb, SUBLANES, GDN_CONV_DIM), F32)],
        compiler_params=_cparams("parallel", "arbitrary"),
        name="gdn",
    )(head_params, p3, p3, p3, p3, p3, conv_w, norm_g.reshape(1, GDN_DV), buf8, s0)


def _rope_kernel(cos_ref, sin_ref, *, past_len):
    rows, half = cos_ref.shape
    rid = lax.broadcasted_iota(jnp.int32, (rows, 1), 0)
    pos = (rid + (pl.program_id(0) * rows + past_len)).astype(F32)
    fidx = lax.broadcasted_iota(jnp.int32, (1, half), 1).astype(F32)
    inv = jnp.exp(fidx * (-math.log(ROPE_BASE) / half))
    ang = pos * inv
    cos_ref[...] = jnp.cos(ang)
    sin_ref[...] = jnp.sin(ang)


def _rope_tables(tp, half, past_len):
    rows = _pick_tile(tp, 512, SUBLANES)
    shape = jax.ShapeDtypeStruct((tp, half), F32)
    spec = pl.BlockSpec((rows, half), lambda i: (i, 0))
    return pl.pallas_call(functools.partial(_rope_kernel, past_len=past_len), out_shape=(shape, shape),
                          grid=(tp // rows,), out_specs=(spec, spec), name="rope")()


def _ret_kernel(q_ref, k_ref, v_ref, z_ref, ng_ref, s0_ref, cos_ref, sin_ref, o_ref, sout_ref, s_ref, *,
                t_true, c_true, padded):
    n = pl.program_id(1)
    c = q_ref.shape[1]
    dk = q_ref.shape[2] // RET_HEADS
    dv = v_ref.shape[2] // RET_HEADS
    half = dk // 2

    @pl.when(n == 0)
    def _():
        s_ref[...] = s0_ref[0]

    rid = lax.broadcasted_iota(jnp.int32, (c, 1), 0)
    valid = ((rid + n * c) < t_true).astype(F32) if padded else None
    cos, sin = cos_ref[...], sin_ref[...]
    ii = lax.broadcasted_iota(jnp.int32, (c, c), 0)
    jj = lax.broadcasted_iota(jnp.int32, (c, c), 1)
    diff = (ii - jj).astype(F32)
    idx = rid.astype(F32)

    def rot(x):
        x1, x2 = x[:, :half], x[:, half:]
        return jnp.concatenate([x1 * cos - x2 * sin, x2 * cos + x1 * sin], axis=-1)

    def head(h):
        lg = math.log1p(-(2.0 ** (-5.0 - h)))
        qh = _mask_rows(rot(q_ref[0, :, h * dk:(h + 1) * dk]), valid)
        kh = _mask_rows(rot(k_ref[0, :, h * dk:(h + 1) * dk]) * (dk ** -0.5), valid)
        vh = _mask_rows(v_ref[0, :, h * dv:(h + 1) * dv], valid)
        dmask = jnp.where(ii >= jj, jnp.exp(lg * jnp.maximum(diff, 0.0)), 0.0)
        xi = jnp.exp(lg * (idx + 1.0))
        zeta = jnp.exp(lg * (c_true - 1.0 - idx))
        s_old = s_ref[h]
        inner = _dot_nt(qh, kh)
        yield
        o_state = _dot(qh * xi, s_old)
        yield
        o = _dot(inner * dmask, vh) + o_state
        yield
        s_ref[h] = math.exp(lg * c_true) * s_old + _dot_tn(kh * zeta, vh)
        yield
        o = _rms(o, ng_ref[...]) * _silu(z_ref[0, :, h * dv:(h + 1) * dv])
        o_ref[0, :, h * dv:(h + 1) * dv] = o

    chains = [head(h) for h in range(RET_HEADS)]
    for _ in zip(*chains):
        pass
    for chain in chains:
        for _ in chain:
            pass

    @pl.when(n == pl.num_programs(1) - 1)
    def _():
        sout_ref[0] = s_ref[...]


def _retention(p3, norm_g, s0, t_true, past_len):
    b, tp, width = p3.shape
    d = width // 6
    dk, dv = d // RET_HEADS, 2 * d // RET_HEADS
    c = min(RET_CHUNK, tp)
    kern = functools.partial(_ret_kernel, t_true=t_true, c_true=min(RET_CHUNK, t_true), padded=t_true != tp)
    cos, sin = _rope_tables(tp, dk // 2, past_len)
    return pl.pallas_call(
        kern,
        out_shape=(jax.ShapeDtypeStruct((b, tp, 2 * d), F32),
                   jax.ShapeDtypeStruct((b, RET_HEADS, dk, dv), F32)),
        grid=(b, tp // c),
        in_specs=[pl.BlockSpec((1, c, d), lambda bi, n: (bi, n, 0)),
                  pl.BlockSpec((1, c, d), lambda bi, n: (bi, n, 1)),
                  pl.BlockSpec((1, c, 2 * d), lambda bi, n: (bi, n, 1)),
                  pl.BlockSpec((1, c, 2 * d), lambda bi, n: (bi, n, 2)),
                  pl.BlockSpec((1, dv), lambda bi, n: (0, 0)),
                  pl.BlockSpec((1, RET_HEADS, dk, dv), lambda bi, n: (bi, 0, 0, 0)),
                  pl.BlockSpec((c, dk // 2), lambda bi, n: (n, 0)),
                  pl.BlockSpec((c, dk // 2), lambda bi, n: (n, 0))],
        out_specs=(pl.BlockSpec((1, c, 2 * d), lambda bi, n: (bi, n, 0)),
                   pl.BlockSpec((1, RET_HEADS, dk, dv), lambda bi, n: (bi, 0, 0, 0))),
        scratch_shapes=[pltpu.VMEM((RET_HEADS, dk, dv), F32)],
        compiler_params=_cparams("parallel", "arbitrary"),
        name="retention",
    )(p3, p3, p3, p3, norm_g.reshape(1, dv), s0, cos, sin)


def _post_kernel(*refs, n_mix, final, k_row, v_row):
    x_ref = refs[0]
    mix_refs = refs[1:1 + n_mix]
    w_refs = refs[1 + n_mix:1 + 2 * n_mix]
    gxa_ref, wq_ref, mk_ref, mv_ref, wo_ref, gf_ref, o_ref = refs[1 + 2 * n_mix:]
    bs, tm, d = x_ref.shape
    rows = bs * tm
    y = x_ref[...].reshape(rows, d)
    for m_ref, w_ref in zip(mix_refs, w_refs):
        y = y + jnp.dot(m_ref[...].reshape(rows, m_ref.shape[2]).astype(BF16), w_ref[...],
                        preferred_element_type=F32)
    h = _rms(y, gxa_ref[...])
    q = jnp.dot(h.astype(BF16), wq_ref[...], preferred_element_type=F32)
    hd = d // XA_HEADS
    out = [[None] * XA_HEADS for _ in range(bs)]

    def attend(s, i):
        qs = q[s * tm:(s + 1) * tm, i * hd:(i + 1) * hd]
        sc = _dot_nt(qs, mk_ref[s, k_row + i]) * (hd ** -0.5)
        yield
        sc = sc - jnp.max(sc, axis=-1, keepdims=True)
        p = jnp.exp(sc)
        p = p / jnp.sum(p, axis=-1, keepdims=True)
        out[s][i] = _dot(p, mv_ref[s, v_row + i])
        yield

    chains = [attend(s, i) for s in range(bs) for i in range(XA_HEADS)]
    for _ in zip(*chains):
        pass
    o = jnp.concatenate([jnp.concatenate(out[s], axis=-1) for s in range(bs)], axis=0)
    y = y + jnp.dot(o.astype(BF16), wo_ref[...], preferred_element_type=F32)
    if final:
        y = _rms(y, gf_ref[...])
    o_ref[...] = y.reshape(bs, tm, d)


POST_ROWS = 256
POST_SEQS_SHORT = 4


def _post(x, mixes, g_xa, wq, mem_k, mem_v, wo, g_final, final):
    b, tp, d = x.shape
    tm = min(tp, POST_ROWS)
    bs = math.gcd(b, POST_SEQS_SHORT) if tp == SUBLANES else 1
    (mk, k_lead, k_row), (mv, v_lead, v_row) = mem_k, mem_v
    row = lambda bi, i: (bi, i, 0)
    const = lambda bi, i: (0, 0)
    in_specs = [pl.BlockSpec((bs, tm, d), row)]
    in_specs += [pl.BlockSpec((bs, tm, a.shape[2]), row) for a, _ in mixes]
    in_specs += [pl.BlockSpec(w.shape, const) for _, w in mixes]
    in_specs += [pl.BlockSpec((1, d), const), pl.BlockSpec((d, d), const),
                 pl.BlockSpec((None, bs) + mk.shape[2:], lambda bi, i: (k_lead, bi, 0, 0, 0)),
                 pl.BlockSpec((None, bs) + mv.shape[2:], lambda bi, i: (v_lead, bi, 0, 0, 0)),
                 pl.BlockSpec((d, d), const), pl.BlockSpec((1, d), const)]
    mem_k, mem_v = mk, mv
    return pl.pallas_call(
        functools.partial(_post_kernel, n_mix=len(mixes), final=final, k_row=k_row, v_row=v_row),
        out_shape=jax.ShapeDtypeStruct((b, tp, d), F32),
        grid=(b // bs, tp // tm),
        in_specs=in_specs,
        out_specs=pl.BlockSpec((bs, tm, d), row),
        compiler_params=_cparams("parallel", "arbitrary"),
        name="post",
    )(x, *[a for a, _ in mixes], *[w for _, w in mixes], g_xa.reshape(1, d), wq, mem_k, mem_v, wo,
      g_final.reshape(1, d))


def _prep_weights(w_in_ab, w_out_ab, w_in_c, w_out_c, xa_wq, xa_wo):
    n_ab_pad = -(-AB_IN // LANES) * LANES
    return dict(
        w_in_ab=jnp.pad(w_in_ab, ((0, 0), (0, 0), (0, n_ab_pad - AB_IN))).astype(BF16),
        w_out_ab=w_out_ab.astype(BF16), w_in_c=w_in_c.astype(BF16), w_out_c=w_out_c.astype(BF16),
        xa_wq=xa_wq.astype(BF16), xa_wo=xa_wo.astype(BF16))


def _trunk(x, past_len, sb_fn, conv_bufs, gdn_states, ret_states, mem_k, mem_v, wts,
           norm_mix_g, norm_xa_g, sb_logit_bias, conv_w_gdn, gdn_a_log, gdn_dt_bias, gdn_norm_g,
           ret_norm_g, norm_final_g):
    b, t, d = x.shape
    tp = -(-t // SUBLANES) * SUBLANES
    if tp != t:
        x = jnp.pad(x, ((0, 0), (0, tp - t), (0, 0)))
    depth = norm_mix_g.shape[0]
    ks, vs, convs, gdns, rets = [], [], [], [], []
    for layer in range(depth):
        i = layer // 2
        last = layer == depth - 1
        if layer % 2 == 0:
            p, pb = _proj(x.reshape(b * tp, d), norm_mix_g[layer], wts["w_in_ab"][i], norm=True, emit_bf16=True)
            p = p.reshape(b, tp, -1)
            o_sb = sb_fn(i, p, pb.reshape(b, tp, -1))
            buf8 = jnp.pad(conv_bufs[i], ((0, 0), (SUBLANES - (CONV_W - 1), 0), (0, 0)))
            o_g, s_gdn = _gdn(p, conv_w_gdn[i], gdn_a_log[i], gdn_dt_bias[i], gdn_norm_g[i], buf8, gdn_states[i], t)
            ks.append(p[:, :t, SB_WIDTH:2 * SB_WIDTH].reshape(b, t, SB_HEADS, SB_HEAD_DIM))
            vs.append(p[:, :t, 2 * SB_WIDTH:3 * SB_WIDTH].reshape(b, t, SB_HEADS, SB_HEAD_DIM))
            qkv = p[:, :t, 4 * SB_WIDTH:4 * SB_WIDTH + GDN_CONV_DIM]
            keep = CONV_W - 1
            convs.append(jnp.concatenate([conv_bufs[i], qkv[:, max(t - keep, 0):]], axis=1)[:, -keep:])
            gdns.append(s_gdn)
            mixes = [(o_sb, wts["w_out_ab"][i, :SB_WIDTH]), (o_g, wts["w_out_ab"][i, SB_WIDTH:])]
        else:
            p = _proj(x.reshape(b * tp, d), norm_mix_g[layer], wts["w_in_c"][i], norm=True).reshape(b, tp, -1)
            o_r, s_ret = _retention(p, ret_norm_g[i], ret_states[i], t, past_len)
            rets.append(s_ret)
            mixes = [(o_r, wts["w_out_c"][i])]
        x = _post(x, mixes, norm_xa_g[layer], wts["xa_wq"][layer], mem_k(layer), mem_v(layer),
                  wts["xa_wo"][layer], norm_final_g, last)
    return (x[:, :t], jnp.stack(ks), jnp.stack(vs), jnp.stack(gdns), jnp.stack(convs), jnp.stack(rets))


def kernel(x_prompt, x_sample, mem_prompt, cache_sb_k, cache_sb_v, state_gdn, state_gdn_conv, state_ret,
           cache_mem_k, cache_mem_v, page_table, norm_mix_g, norm_xa_g, w_in_ab, sb_logit_bias, conv_w_gdn,
           gdn_a_log, gdn_dt_bias, gdn_norm_g, w_out_ab, w_in_c, ret_norm_g, w_out_c, xa_wq, xa_wk, xa_wv,
           xa_wo, norm_final_g):
    wts = _prep_weights(w_in_ab, w_out_ab, w_in_c, w_out_c, xa_wq, xa_wo)
    shared = (norm_mix_g, norm_xa_g, sb_logit_bias, conv_w_gdn, gdn_a_log, gdn_dt_bias, gdn_norm_g,
              ret_norm_g, norm_final_g)
    depth = norm_mix_g.shape[0]
    n_ab, n_c = state_gdn.shape[0], state_ret.shape[0]

    bp, mlen, d = mem_prompt.shape
    hd = d // XA_HEADS
    w_kv = jnp.concatenate([xa_wk, xa_wv], axis=0).astype(BF16)
    w_kv = jnp.moveaxis(w_kv, 0, 1).reshape(d, 2 * depth * d)
    mem_kv = _proj(mem_prompt.reshape(bp * mlen, d), jnp.ones((d,), F32), w_kv, norm=False)
    mem_kv = mem_kv.reshape(bp, mlen, 2 * depth, XA_HEADS, hd)
    mem_rows = jnp.transpose(mem_kv.reshape(bp, mlen, 2 * depth * XA_HEADS, hd), (0, 2, 1, 3))[None]
    mem_kv = jnp.moveaxis(mem_kv, 2, 0)
    mem_k_prompt, mem_v_prompt = mem_kv[:depth], mem_kv[depth:]
    conv0 = jnp.zeros((n_ab, bp) + state_gdn_conv.shape[2:], F32)
    gdn0 = jnp.zeros((n_ab, bp) + state_gdn.shape[2:], F32)
    ret0 = jnp.zeros((n_c, bp) + state_ret.shape[2:], F32)
    prompt = _trunk(x_prompt, 0, lambda i, p, pb: _sb_prompt(p, pb, sb_logit_bias[i]), conv0, gdn0, ret0,
                    lambda l: (mem_rows, 0, l * XA_HEADS), lambda l: (mem_rows, 0, (depth + l) * XA_HEADS),
                    wts, *shared)

    n_pages, page = page_table.shape[1], cache_sb_k.shape[2]
    assert x_sample.shape[1] == 1
    n_phys = cache_sb_k.shape[1]
    ckt = jnp.transpose(cache_sb_k, (0, 1, 3, 4, 2)).reshape(n_ab, n_phys, SB_WIDTH, page)
    cvt = jnp.transpose(cache_sb_v, (0, 1, 3, 4, 2)).reshape(n_ab, n_phys, SB_WIDTH, page)
    smem_k = jnp.transpose(cache_mem_k, (0, 1, 3, 2, 4))
    smem_v = jnp.transpose(cache_mem_v, (0, 1, 3, 2, 4))
    sample = _trunk(x_sample, n_pages * page,
                    lambda i, p, pb: _sb_decode(p, sb_logit_bias[i], ckt, cvt, i, page_table),
                    state_gdn_conv, state_gdn, state_ret,
                    lambda l: (smem_k, l, 0), lambda l: (smem_v, l, 0), wts, *shared)

    y_p, k_p, v_p, gdn_p, conv_p, ret_p = prompt
    y_s, k_s, v_s, gdn_s, conv_s, ret_s = sample
    return (y_p, y_s, k_p, v_p, gdn_p, conv_p, ret_p, mem_k_prompt, mem_v_prompt,
            k_s, v_s, gdn_s, conv_s, ret_s)
```
